```python
import jax, jax.numpy as jnp
from jax import lax
import numpy as np

D_MODEL = 1024
BATCH = 16
SEQ = 2048
DEPTH = 2
DEC_BATCH = 8
DEC_SEQ = 64
PAST_LEN = 2048

CHUNK = 64
NORM_EPS = 1e-6
GLA_HEADS = 4
GLA_DV = D_MODEL // 2 // GLA_HEADS
GLA_DK = GLA_DV // 2
GLA_K = GLA_HEADS * GLA_DK
GLA_V = GLA_HEADS * GLA_DV
GLA_LR = 16
GLA_GATE_NORMALIZER = 16.0
RWKV_N = 64
RWKV_C = D_MODEL // 2
RWKV_HEADS = RWKV_C // RWKV_N
RWKV_W_LR = 64
RWKV_A_LR = 64
RWKV_G_LR = 128
RWKV_GN_EPS = 64e-5
GLA_COLS = 2 * GLA_K + 2 * GLA_V + GLA_LR
RWKV_COLS = 3 * RWKV_C + RWKV_W_LR + RWKV_A_LR + RWKV_G_LR
IN_COLS = GLA_COLS + RWKV_COLS
MIX_WIDTH = GLA_V + RWKV_C
D_FF = 2816
CONV_W = 3

kernel_name = 'hymba_gla_rwkv7_convffn_stream_step'


def _split(x, sizes):
    idx = [int(i) for i in np.cumsum(sizes)[:-1]]
    return jnp.split(x, idx, axis=-1)


def _rmsnorm(x, g):
    xf = x.astype(jnp.float32)
    y = xf * lax.rsqrt(jnp.mean(xf * xf, axis=-1, keepdims=True) + NORM_EPS)
    return (y * g.astype(jnp.float32)).astype(x.dtype)


def _gla_chunked(q, k, v, log_a, s0):
    b_, l_, h_, _ = q.shape
    dv = v.shape[-1]
    c = min(CHUNK, l_)
    n = l_ // c

    def to_blocks(t):
        return t.reshape(b_, n, c, h_, t.shape[-1]).transpose(1, 0, 3, 2, 4)

    causal = jnp.tril(jnp.ones((c, c), dtype=bool))

    def step(s, inp):
        qc, kc, vc, lac = inp
        cum = jnp.cumsum(lac, axis=2)
        last = cum[:, :, -1:, :]
        q_e = qc * jnp.exp(cum)
        k_e = kc * jnp.exp(-cum)
        scores = jnp.where(causal, jnp.einsum('bhtk,bhsk->bhts', q_e, k_e), 0.0)
        o = jnp.einsum('bhts,bhsv->bhtv', scores, vc) + jnp.einsum('bhtk,bhkv->bhtv', q_e, s)
        s_new = s * jnp.exp(last[:, :, 0, :])[..., None] + jnp.einsum(
            'bhsk,bhsv->bhkv', kc * jnp.exp(last - cum), vc)
        return s_new, o

    s_fin, o = lax.scan(step, s0, (to_blocks(q), to_blocks(k), to_blocks(v), to_blocks(log_a)))
    o = o.transpose(1, 0, 3, 2, 4).reshape(b_, l_, h_, dv)
    return o, s_fin


def _rwkv7_scan(r, w, k, v, a_vec, b_vec, s0):
    def step(s, inp):
        rt, wt, kt, vt, at, bt = inp
        sa = jnp.einsum('bhvk,bhk->bhv', s, at)
        s = s * wt[:, :, None, :] + sa[..., None] * bt[:, :, None, :] + vt[..., None] * kt[:, :, None, :]
        y = jnp.einsum('bhvk,bhk->bhv', s, rt)
        return s, y

    xs = tuple(t.transpose(1, 0, 2, 3) for t in (r, w, k, v, a_vec, b_vec))
    s_fin, y = lax.scan(step, s0, xs)
    return y.transpose(1, 0, 2, 3), s_fin


def _token_mixers(h, s_gla, s_rwkv, shift_prev, p, l):
    f32 = jnp.float32
    bsz, seq, _ = h.shape
    proj = jnp.einsum('bld,dc->blc', h, p['w_in'][l])
    p_gla, p_rwkv = proj[..., :GLA_COLS], proj[..., GLA_COLS:]

    gq, gk, gv, gg, glr = _split(p_gla, (GLA_K, GLA_K, GLA_V, GLA_V, GLA_LR))
    q = gq.astype(f32).reshape(bsz, seq, GLA_HEADS, GLA_DK) * (GLA_DK ** -0.5)
    k = gk.astype(f32).reshape(bsz, seq, GLA_HEADS, GLA_DK)
    v = gv.astype(f32).reshape(bsz, seq, GLA_HEADS, GLA_DV)
    gate_logit = (glr @ p['gla_gate_w2'][l] + p['gla_gate_b'][l]).astype(f32)
    log_a = (jax.nn.log_sigmoid(gate_logit) / GLA_GATE_NORMALIZER).reshape(bsz, seq, GLA_HEADS, GLA_DK)
    o_gla, s_gla_new = _gla_chunked(q, k, v, log_a, s_gla.astype(f32))
    o_gla = o_gla * lax.rsqrt(jnp.mean(o_gla * o_gla, axis=-1, keepdims=True) + NORM_EPS) * p['gla_norm_w'][l]
    o_gla = o_gla.reshape(bsz, seq, GLA_V) * jax.nn.silu(gg.astype(f32))

    prev = jnp.concatenate([shift_prev[:, None, :].astype(p_rwkv.dtype), p_rwkv[:, :-1]], axis=1)
    xs = p_rwkv + p['rwkv_mu'][l] * (prev - p_rwkv)
    shift_new = p_rwkv[:, -1]
    xr, xk, xv, xw, xa, xg = _split(xs, (RWKV_C, RWKV_C, RWKV_C, RWKV_W_LR, RWKV_A_LR, RWKV_G_LR))
    w_raw = (p['rwkv_w0'][l] + jnp.tanh(xw) @ p['rwkv_w2'][l]).astype(f32)
    w_log = -jnp.exp(-jax.nn.softplus(-w_raw) - 0.5)
    a = jax.nn.sigmoid((p['rwkv_a0'][l] + xa @ p['rwkv_a2'][l]).astype(f32))
    g = (jax.nn.sigmoid(xg) @ p['rwkv_g2'][l]).astype(f32)
    r = xr.astype(f32)
    kr = xk.astype(f32)
    vr = xv.astype(f32)
    kk = (kr * p['rwkv_k_k'][l]).reshape(bsz, seq, RWKV_HEADS, RWKV_N)
    kk = kk / jnp.maximum(jnp.sqrt(jnp.sum(kk * kk, axis=-1, keepdims=True)), 1e-12)
    kr = kr * (1.0 + (a - 1.0) * p['rwkv_k_a'][l])
    heads = lambda t: t.reshape(bsz, seq, RWKV_HEADS, RWKV_N)
    rh, kh, vh, ah = heads(r), heads(kr), heads(vr), heads(a)
    y, s_rwkv_new = _rwkv7_scan(rh, jnp.exp(heads(w_log)), kh, vh, -kk, kk * ah, s_rwkv.astype(f32))
    mu = jnp.mean(y, axis=-1, keepdims=True)
    var = jnp.mean(jnp.square(y - mu), axis=-1, keepdims=True)
    y = ((y - mu) * lax.rsqrt(var + RWKV_GN_EPS)).reshape(bsz, seq, RWKV_C)
    y = y * p['rwkv_ln_w'][l] + p['rwkv_ln_b'][l]
    bonus = jnp.sum(rh * kh * p['rwkv_r_k'][l], axis=-1, keepdims=True) * vh
    y = (y + bonus.reshape(bsz, seq, RWKV_C)) * g

    mixed = jnp.concatenate([o_gla, y], axis=-1).astype(h.dtype)
    out = jnp.einsum('blc,cd->bld', mixed, p['w_out'][l])
    return out, s_gla_new, s_rwkv_new, shift_new


def _conv_ffn(h, conv_prev, w_up, conv_w, conv_b, w_down):
    seq = h.shape[1]
    uv = jnp.einsum('bld,df->blf', h, w_up)
    u, gate = uv[..., :D_FF], uv[..., D_FF:]
    u_ext = jnp.concatenate([conv_prev.astype(u.dtype), u], axis=1)
    uc = conv_b + sum(conv_w[j] * u_ext[:, j:j + seq] for j in range(CONV_W))
    conv_new = u_ext[:, seq:]
    y = jnp.einsum('blf,fd->bld', jax.nn.gelu(uc, approximate=False) * gate, w_down)
    return y, conv_new


def _trunk(x, s_gla, s_rwkv, s_shift, c_conv, p):
    new_gla, new_rwkv, new_shift, new_conv = [], [], [], []
    for l in range(DEPTH):
        mix, sg, sr, ss = _token_mixers(_rmsnorm(x, p['norm_mix'][l]), s_gla[l], s_rwkv[l], s_shift[l], p, l)
        x = x + mix
        f, cc = _conv_ffn(_rmsnorm(x, p['norm_ffn'][l]), c_conv[l], p['ffn_w_up'][l], p['ffn_conv_w'][l],
                          p['ffn_conv_b'][l], p['ffn_w_down'][l])
        x = x + f
        new_gla.append(sg.astype(s_gla.dtype))
        new_rwkv.append(sr.astype(s_rwkv.dtype))
        new_shift.append(ss.astype(s_shift.dtype))
        new_conv.append(cc.astype(c_conv.dtype))
    y = _rmsnorm(x, p['norm_final'])
    return y, jnp.stack(new_gla), jnp.stack(new_rwkv), jnp.stack(new_shift), jnp.stack(new_conv)


def setup_inputs(seed: int = 0) -> dict:
    key = jax.random.key(seed)
    ks = jax.random.split(key, 30)
    f32 = jnp.float32

    def nrm(k, shape, s):
        return jax.random.normal(k, shape, f32) * s

    return {
        'x_prompt': nrm(ks[0], (BATCH, SEQ, D_MODEL), 1.0),
        'x_sample': nrm(ks[1], (DEC_BATCH, DEC_SEQ, D_MODEL), 1.0),
        'state_gla': nrm(ks[2], (DEPTH, DEC_BATCH, GLA_HEADS, GLA_DK, GLA_DV), 0.1),
        'state_rwkv': nrm(ks[3], (DEPTH, DEC_BATCH, RWKV_HEADS, RWKV_N, RWKV_N), 0.1),
        'state_rwkv_shift': nrm(ks[4], (DEPTH, DEC_BATCH, RWKV_COLS), 1.0),
        'cache_ffn_conv': nrm(ks[5], (DEPTH, DEC_BATCH, CONV_W - 1, D_FF), 1.0),
        'norm_mix': 1.0 + nrm(ks[6], (DEPTH, D_MODEL), 0.02),
        'w_in': nrm(ks[7], (DEPTH, D_MODEL, IN_COLS), D_MODEL ** -0.5),
        'gla_gate_w2': nrm(ks[8], (DEPTH, GLA_LR, GLA_K), GLA_LR ** -0.5),
        'gla_gate_b': nrm(ks[9], (DEPTH, GLA_K), 0.1),
        'gla_norm_w': 1.0 + nrm(ks[10], (DEPTH, GLA_DV), 0.02),
        'rwkv_mu': jax.random.uniform(ks[11], (DEPTH, RWKV_COLS), f32),
        'rwkv_w0': nrm(ks[12], (DEPTH, RWKV_C), 0.5),
        'rwkv_w2': nrm(ks[13], (DEPTH, RWKV_W_LR, RWKV_C), 0.1),
        'rwkv_a0': nrm(ks[14], (DEPTH, RWKV_C), 0.1),
        'rwkv_a2': nrm(ks[15], (DEPTH, RWKV_A_LR, RWKV_C), 0.1),
        'rwkv_g2': nrm(ks[16], (DEPTH, RWKV_G_LR, RWKV_C), RWKV_G_LR ** -0.5),
        'rwkv_k_k': 0.85 + nrm(ks[17], (DEPTH, RWKV_C), 0.05),
        'rwkv_k_a': 1.0 + nrm(ks[18], (DEPTH, RWKV_C), 0.05),
        'rwkv_r_k': nrm(ks[19], (DEPTH, RWKV_HEADS, RWKV_N), 0.1),
        'rwkv_ln_w': 1.0 + nrm(ks[20], (DEPTH, RWKV_C), 0.02),
        'rwkv_ln_b': nrm(ks[21], (DEPTH, RWKV_C), 0.01),
        'w_out': nrm(ks[22], (DEPTH, MIX_WIDTH, D_MODEL), MIX_WIDTH ** -0.5),
        'norm_ffn': 1.0 + nrm(ks[23], (DEPTH, D_MODEL), 0.02),
        'ffn_w_up': nrm(ks[24], (DEPTH, D_MODEL, 2 * D_FF), D_MODEL ** -0.5),
        'ffn_conv_w': nrm(ks[25], (DEPTH, CONV_W, D_FF), CONV_W ** -0.5),
        'ffn_conv_b': nrm(ks[26], (DEPTH, D_FF), 0.01),
        'ffn_w_down': nrm(ks[27], (DEPTH, D_FF, D_MODEL), D_FF ** -0.5),
        'norm_final': 1.0 + nrm(ks[28], (D_MODEL,), 0.02),
    }


def reference(x_prompt, x_sample, state_gla, state_rwkv, state_rwkv_shift, cache_ffn_conv,
              norm_mix, w_in, gla_gate_w2, gla_gate_b, gla_norm_w, rwkv_mu, rwkv_w0, rwkv_w2,
              rwkv_a0, rwkv_a2, rwkv_g2, rwkv_k_k, rwkv_k_a, rwkv_r_k, rwkv_ln_w, rwkv_ln_b,
              w_out, norm_ffn, ffn_w_up, ffn_conv_w, ffn_conv_b, ffn_w_down, norm_final):
    params = dict(norm_mix=norm_mix, w_in=w_in, gla_gate_w2=gla_gate_w2, gla_gate_b=gla_gate_b,
                  gla_norm_w=gla_norm_w, rwkv_mu=rwkv_mu, rwkv_w0=rwkv_w0, rwkv_w2=rwkv_w2,
                  rwkv_a0=rwkv_a0, rwkv_a2=rwkv_a2, rwkv_g2=rwkv_g2, rwkv_k_k=rwkv_k_k,
                  rwkv_k_a=rwkv_k_a, rwkv_r_k=rwkv_r_k, rwkv_ln_w=rwkv_ln_w, rwkv_ln_b=rwkv_ln_b,
                  w_out=w_out, norm_ffn=norm_ffn, ffn_w_up=ffn_w_up, ffn_conv_w=ffn_conv_w,
                  ffn_conv_b=ffn_conv_b, ffn_w_down=ffn_w_down, norm_final=norm_final)
    nb = x_prompt.shape[0]
    g0 = jnp.zeros((DEPTH, nb, GLA_HEADS, GLA_DK, GLA_DV), state_gla.dtype)
    r0 = jnp.zeros((DEPTH, nb, RWKV_HEADS, RWKV_N, RWKV_N), state_rwkv.dtype)
    sh0 = jnp.zeros((DEPTH, nb, RWKV_COLS), state_rwkv_shift.dtype)
    c0 = jnp.zeros((DEPTH, nb, CONV_W - 1, D_FF), cache_ffn_conv.dtype)
    y_prompt, gla_p, rwkv_p, shift_p, conv_p = _trunk(x_prompt, g0, r0, sh0, c0, params)
    y_sample, gla_s, rwkv_s, shift_s, conv_s = _trunk(
        x_sample, state_gla, state_rwkv, state_rwkv_shift, cache_ffn_conv, params)
    return (y_prompt, y_sample, gla_p, rwkv_p, shift_p, conv_p, gla_s, rwkv_s, shift_s, conv_s)
```

```python
import functools

import jax
import jax.numpy as jnp
from jax import lax
from jax.experimental import pallas as pl
from jax.experimental.pallas import tpu as pltpu

F32 = jnp.float32
BF16 = jnp.bfloat16

D_MODEL = 1024
DEPTH = 2
CHUNK = 64
NORM_EPS = 1e-6
GLA_HEADS = 4
GLA_DV = 128
GLA_DK = 64
GLA_K = GLA_HEADS * GLA_DK
GLA_V = GLA_HEADS * GLA_DV
GLA_LR = 16
GLA_GATE_NORMALIZER = 16.0
RWKV_N = 64
RWKV_C = 512
RWKV_HEADS = 8
RWKV_PAIRS = RWKV_HEADS // 2
RWKV_W_LR = 64
RWKV_A_LR = 64
RWKV_G_LR = 128
RWKV_GN_EPS = 64e-5
GLA_COLS = 2 * GLA_K + 2 * GLA_V + GLA_LR
RWKV_COLS = 3 * RWKV_C + RWKV_W_LR + RWKV_A_LR + RWKV_G_LR
D_FF = 2816
CONV_W = 3

LANES = 128
GLA_SLAB = 2 * GLA_K + 2 * GLA_V + LANES
VMEM_LIMIT_BYTES = 56 * 1024 * 1024

_GQ, _GK, _GV, _GG, _GLR = 0, GLA_K, 2 * GLA_K, 2 * GLA_K + GLA_V, 2 * GLA_K + 2 * GLA_V
_RR, _RK, _RV, _RWA, _RG = 0, RWKV_C, 2 * RWKV_C, 3 * RWKV_C, 3 * RWKV_C + RWKV_W_LR + RWKV_A_LR


def _bdot(a, b):
    return jnp.dot(a.astype(BF16), b.astype(BF16), preferred_element_type=F32)


def _bdot_nt(a, b):
    return lax.dot_general(a.astype(BF16), b.astype(BF16), (((1,), (1,)), ((), ())),
                           preferred_element_type=F32)


def _bdot_tn(a, b):
    return lax.dot_general(a.astype(BF16), b.astype(BF16), (((0,), (0,)), ((), ())),
                           preferred_element_type=F32)


def _split3(x):
    hi = x.astype(BF16)
    r1 = x - hi.astype(F32)
    mid = r1.astype(BF16)
    lo = (r1 - mid.astype(F32)).astype(BF16)
    return hi, mid, lo


def _exact_dot(m_bf16, x):
    hi, mid, lo = _split3(x)
    d = lambda p: jnp.dot(m_bf16, p, preferred_element_type=F32)
    return d(hi) + d(mid) + d(lo)


def _exact_dot_r(x, m_bf16):
    hi, mid, lo = _split3(x)
    d = lambda p: jnp.dot(p, m_bf16, preferred_element_type=F32)
    return d(hi) + d(mid) + d(lo)


def _tril_incl(n):
    r = lax.broadcasted_iota(jnp.int32, (n, n), 0)
    c = lax.broadcasted_iota(jnp.int32, (n, n), 1)
    return (c <= r)


def _rmsnorm(x, g):
    return x * lax.rsqrt(jnp.mean(x * x, axis=-1, keepdims=True) + NORM_EPS) * g


def _shift_rows(x, k, fill_rows):
    y = pltpu.roll(x, k, 0)
    row = lax.broadcasted_iota(jnp.int32, x.shape, 0)
    for i, f in enumerate(fill_rows):
        y = jnp.where(row == i, f, y)
    return y


def _inproj_kernel(x_ref, g_ref, wg_ref, wr_ref, og_ref, or_ref):
    h = _rmsnorm(x_ref[...], g_ref[...]).astype(BF16)
    og_ref[...] = jnp.dot(h, wg_ref[...], preferred_element_type=F32)
    or_ref[...] = jnp.dot(h, wr_ref[...], preferred_element_type=F32)


def _inproj(x2d, g, w_gla, w_rwkv, tm):
    t = x2d.shape[0]
    return pl.pallas_call(
        _inproj_kernel,
        grid=(t // tm,),
        in_specs=[
            pl.BlockSpec((tm, D_MODEL), lambda i: (i, 0)),
            pl.BlockSpec((1, D_MODEL), lambda i: (0, 0)),
            pl.BlockSpec((D_MODEL, GLA_SLAB), lambda i: (0, 0)),
            pl.BlockSpec((D_MODEL, RWKV_COLS), lambda i: (0, 0)),
        ],
        out_specs=[
            pl.BlockSpec((tm, GLA_SLAB), lambda i: (i, 0)),
            pl.BlockSpec((tm, RWKV_COLS), lambda i: (i, 0)),
        ],
        out_shape=[
            jax.ShapeDtypeStruct((t, GLA_SLAB), F32),
            jax.ShapeDtypeStruct((t, RWKV_COLS), F32),
        ],
        compiler_params=pltpu.CompilerParams(
            dimension_semantics=("arbitrary",), vmem_limit_bytes=VMEM_LIMIT_BYTES),
        name="inproj",
    )(x2d, g, w_gla, w_rwkv)


def _gla_kernel(p_ref, w2_ref, b_ref, nw_ref, s0_ref, o_ref, sout_ref, st_ref, *, n_chunks):
    ti = pl.program_id(1)

    @pl.when(ti == 0)
    def _():
        st_ref[...] = s0_ref[0]

    tril = _tril_incl(CHUNK)
    tril_bf = tril.astype(BF16)
    lane = lax.broadcasted_iota(jnp.int32, (CHUNK, LANES), 1)
    half = [lane < GLA_DK, lane >= GLA_DK]
    w2 = w2_ref[...]
    gate_b = b_ref[...]
    nw = nw_ref[...]

    def body(c, carry):
        rows = pl.ds(pl.multiple_of(c * CHUNK, CHUNK), CHUNK)
        logit = _bdot(p_ref[0, rows, _GLR:_GLR + LANES], w2) + gate_b
        la = jax.nn.log_sigmoid(logit) / GLA_GATE_NORMALIZER
        cum = _exact_dot(tril_bf, la)
        last = cum[CHUNK - 1:CHUNK, :]
        q = p_ref[0, rows, _GQ:_GQ + GLA_K] * (GLA_DK ** -0.5)
        k = p_ref[0, rows, _GK:_GK + GLA_K]
        q_e = q * jnp.exp(cum)
        k_e = k * jnp.exp(-cum)
        k_hat = k * jnp.exp(last - cum)
        dec = jnp.exp(last)
        for h in range(GLA_HEADS):
            pr = slice((h // 2) * LANES, (h // 2 + 1) * LANES)
            m = half[h % 2]
            v = p_ref[0, rows, _GV + h * GLA_DV:_GV + (h + 1) * GLA_DV]
            gg = p_ref[0, rows, _GG + h * GLA_DV:_GG + (h + 1) * GLA_DV]
            sc = _bdot_nt(jnp.where(m, q_e[:, pr], 0.0), k_e[:, pr])
            sc = jnp.where(tril, sc, 0.0)
            st = st_ref[h]
            o = _bdot(sc, v) + _bdot_nt(q_e[:, pr], st)
            st_ref[h] = st * dec[:, pr] + _bdot_tn(v, jnp.where(m, k_hat[:, pr], 0.0))
            o = o * lax.rsqrt(jnp.mean(o * o, axis=-1, keepdims=True) + NORM_EPS) * nw
            o = o * (gg * jax.nn.sigmoid(gg))
            o_ref[0, rows, h * GLA_DV:(h + 1) * GLA_DV] = o.astype(o_ref.dtype)
        return carry

    lax.fori_loop(0, n_chunks, body, 0)
    sout_ref[0] = st_ref[...]


def _gla(p_gla, w2p, gate_b, norm_w, st0, tl):
    b, l, _ = p_gla.shape
    return pl.pallas_call(
        functools.partial(_gla_kernel, n_chunks=tl // CHUNK),
        grid=(b, l // tl),
        in_specs=[
            pl.BlockSpec((1, tl, GLA_SLAB), lambda i, j: (i, j, 0)),
            pl.BlockSpec((LANES, GLA_K), lambda i, j: (0, 0)),
            pl.BlockSpec((1, GLA_K), lambda i, j: (0, 0)),
            pl.BlockSpec((1, GLA_DV), lambda i, j: (0, 0)),
            pl.BlockSpec((1, GLA_HEADS, GLA_DV, LANES), lambda i, j: (i, 0, 0, 0)),
        ],
        out_specs=[
            pl.BlockSpec((1, tl, GLA_V), lambda i, j: (i, j, 0)),
            pl.BlockSpec((1, GLA_HEADS, GLA_DV, LANES), lambda i, j: (i, 0, 0, 0)),
        ],
        out_shape=[
            jax.ShapeDtypeStruct((b, l, GLA_V), BF16),
            jax.ShapeDtypeStruct((b, GLA_HEADS, GLA_DV, LANES), F32),
        ],
        scratch_shapes=[pltpu.VMEM((GLA_HEADS, GLA_DV, LANES), F32)],
        compiler_params=pltpu.CompilerParams(
            dimension_semantics=("arbitrary", "arbitrary"), vmem_limit_bytes=VMEM_LIMIT_BYTES),
        name="gla",
    )(p_gla, w2p, gate_b, norm_w, st0)


def _rwkv_kernel(p_ref, sh_ref, mu_ref, w0_ref, w2_ref, a0_ref, a2_ref, g2_ref, kk_ref, ka_ref,
                 rk_ref, lnw_ref, lnb_ref, ones_ref, s0_ref,
                 y_ref, sout_ref, shout_ref,
                 s_ref, prev_ref, r_s, w_s, k_s, v_s, a_s, b_s, y_s, *, n_chunks, tl):
    ti = pl.program_id(1)

    @pl.when(ti == 0)
    def _():
        s_ref[...] = s0_ref[0]
        prev_ref[...] = sh_ref[0]

    ones_bd = ones_ref[...]
    seg_sum = lambda x: _exact_dot_r(x, ones_bd)

    p = p_ref[0]
    prev = _shift_rows(p, 1, [prev_ref[...]])
    prev_ref[...] = p[tl - 1:tl, :]
    shout_ref[0] = p[tl - 1:tl, :]
    xs = p + mu_ref[...] * (prev - p)
    xr = xs[:, _RR:_RR + RWKV_C]
    xk = xs[:, _RK:_RK + RWKV_C]
    xv = xs[:, _RV:_RV + RWKV_C]
    xwa = xs[:, _RWA:_RWA + LANES]
    xg = xs[:, _RG:_RG + RWKV_G_LR]
    w_raw = w0_ref[...] + _bdot(jnp.tanh(xwa), w2_ref[...])
    w_log = -jnp.exp(-jax.nn.softplus(-w_raw) - 0.5)
    a = jax.nn.sigmoid(a0_ref[...] + _bdot(xwa, a2_ref[...]))
    g = _bdot(jax.nn.sigmoid(xg), g2_ref[...])
    kk = xk * kk_ref[...]
    kk = kk / jnp.maximum(jnp.sqrt(seg_sum(kk * kk)), 1e-12)
    kr = xk * (1.0 + (a - 1.0) * ka_ref[...])
    bonus = seg_sum(xr * kr * rk_ref[...]) * xv
    r_s[...] = xr
    w_s[...] = w_log
    k_s[...] = kr
    v_s[...] = xv
    a_s[...] = -kk
    b_s[...] = kk * a

    tril_bf = _tril_incl(CHUNK).astype(BF16)
    lane = lax.broadcasted_iota(jnp.int32, (CHUNK, LANES), 1)
    lo_half = lane < RWKV_N
    ri = lax.broadcasted_iota(jnp.int32, (LANES, LANES), 0)
    ci = lax.broadcasted_iota(jnp.int32, (LANES, LANES), 1)
    strict = (ci & (RWKV_N - 1)) < (ri & (RWKV_N - 1))
    incl = (ci & (RWKV_N - 1)) <= (ri & (RWKV_N - 1))
    same_head = (ri >> 6) == (ci >> 6)
    eye = (ri == ci).astype(F32)

    def stack(x):
        return jnp.concatenate([jnp.where(lo_half, x, 0.0), jnp.where(lo_half, 0.0, x)], axis=0)

    def tile(x):
        return jnp.concatenate([x, x], axis=0)

    def fold(x):
        return jnp.where(lo_half, x[:CHUNK], x[CHUNK:])

    def body(c, carry):
        rows = pl.ds(pl.multiple_of(c * CHUNK, CHUNK), CHUNK)
        lw_all = w_s[rows, :]
        cl_all = _exact_dot(tril_bf, lw_all)
        for pr in range(RWKV_PAIRS):
            cols = slice(pr * LANES, (pr + 1) * LANES)
            lw, cl = lw_all[:, cols], cl_all[:, cols]
            cl_end = cl[CHUNK - 1:CHUNK, :]
            e_neg = jnp.exp(-cl)
            e_end = jnp.exp(cl_end - cl)
            r, k, v = r_s[rows, cols], k_s[rows, cols], v_s[rows, cols]
            av, bv = a_s[rows, cols], b_s[rows, cols]
            a_t = av * jnp.exp(cl - lw)
            r_t = r * jnp.exp(cl)
            b_t, k_t = bv * e_neg, k * e_neg
            b_h, k_h = bv * e_end, k * e_end
            lhs = jnp.concatenate([stack(a_t), stack(r_t)], axis=0)
            rhs = jnp.concatenate([stack(b_t), stack(k_t)], axis=0)
            gm = _bdot_nt(lhs, rhs)
            a_ab = jnp.where(strict, gm[:LANES, :LANES], 0.0)
            a_ak = jnp.where(strict, gm[:LANES, LANES:], 0.0)
            r_b = jnp.where(incl, gm[LANES:, :LANES], 0.0)
            r_k = jnp.where(incl, gm[LANES:, LANES:], 0.0)
            inv = eye + a_ab
            pw = a_ab
            for _ in range(5):
                pw = _bdot(pw, pw)
                inv = inv + _bdot(pw, inv)
            s = s_ref[pr]
            ar_s = _bdot_nt(jnp.concatenate([a_t, r_t], axis=0), s)
            u = fold(_bdot(inv, tile(ar_s[:CHUNK] + fold(_bdot(a_ak, tile(v))))))
            y = ar_s[CHUNK:] + fold(_bdot(jnp.concatenate([r_b, r_k], axis=1),
                                          jnp.concatenate([tile(u), tile(v)], axis=0)))
            upd = _bdot_tn(jnp.concatenate([u, v], axis=0), jnp.concatenate([b_h, k_h], axis=0))
            s_ref[pr] = s * jnp.exp(cl_end) + jnp.where(same_head, upd, 0.0)
            y_s[rows, cols] = y
        return carry

    lax.fori_loop(0, n_chunks, body, 0)
    sout_ref[0] = s_ref[...]

    y = y_s[...]
    mean = seg_sum(y) * (1.0 / RWKV_N)
    yc = y - mean
    var = seg_sum(yc * yc) * (1.0 / RWKV_N)
    y = yc * lax.rsqrt(var + RWKV_GN_EPS) * lnw_ref[...] + lnb_ref[...]
    y_ref[0] = ((y + bonus) * g).astype(y_ref.dtype)


def _rwkv(p_rwkv, shift_prev, prm, s0, tl):
    b, l, _ = p_rwkv.shape
    row = lambda n: pl.BlockSpec((1, n), lambda i, j: (0, 0))
    mat = lambda m, n: pl.BlockSpec((m, n), lambda i, j: (0, 0))
    return pl.pallas_call(
        functools.partial(_rwkv_kernel, n_chunks=tl // CHUNK, tl=tl),
        grid=(b, l // tl),
        in_specs=[
            pl.BlockSpec((1, tl, RWKV_COLS), lambda i, j: (i, j, 0)),
            pl.BlockSpec((1, 1, RWKV_COLS), lambda i, j: (i, 0, 0)),
            row(RWKV_COLS),
            row(RWKV_C), mat(LANES, RWKV_C), row(RWKV_C), mat(LANES, RWKV_C), mat(RWKV_G_LR, RWKV_C),
            row(RWKV_C), row(RWKV_C), row(RWKV_C), row(RWKV_C), row(RWKV_C),
            mat(RWKV_C, RWKV_C),
            pl.BlockSpec((1, RWKV_PAIRS, LANES, LANES), lambda i, j: (i, 0, 0, 0)),
        ],
        out_specs=[
            pl.BlockSpec((1, tl, RWKV_C), lambda i, j: (i, j, 0)),
            pl.BlockSpec((1, RWKV_PAIRS, LANES, LANES), lambda i, j: (i, 0, 0, 0)),
            pl.BlockSpec((1, 1, RWKV_COLS), lambda i, j: (i, 0, 0)),
        ],
        out_shape=[
            jax.ShapeDtypeStruct((b, l, RWKV_C), BF16),
            jax.ShapeDtypeStruct((b, RWKV_PAIRS, LANES, LANES), F32),
            jax.ShapeDtypeStruct((b, 1, RWKV_COLS), F32),
        ],
        scratch_shapes=[
            pltpu.VMEM((RWKV_PAIRS, LANES, LANES), F32),
            pltpu.VMEM((1, RWKV_COLS), F32),
        ] + [pltpu.VMEM((tl, RWKV_C), F32)] * 7,
        compiler_params=pltpu.CompilerParams(
            dimension_semantics=("arbitrary", "arbitrary"), vmem_limit_bytes=VMEM_LIMIT_BYTES),
        name="rwkv",
    )(p_rwkv, shift_prev, prm["mu"], prm["w0"], prm["w2p"], prm["a0"], prm["a2p"], prm["g2"],
      prm["k_k"], prm["k_a"], prm["r_k"], prm["ln_w"], prm["ln_b"], prm["ones_bd"], s0)


def _post_kernel(x_ref, og_ref, yr_ref, wo_ref, nf_ref, wu_ref, wg_ref, cw_ref, cb_ref, wd_ref,
                 cprev_ref, nfin_ref, out_ref, cnew_ref, acc_ref, h_ref, carry_ref,
                 *, tiles_per_seq, n_fc, fc, tm, final_norm):
    i = pl.program_id(0)
    c = pl.program_id(1)

    @pl.when(c == 0)
    def _():
        wo = wo_ref[...]
        mix = (jnp.dot(og_ref[...], wo[:GLA_V], preferred_element_type=F32)
               + jnp.dot(yr_ref[...], wo[GLA_V:], preferred_element_type=F32))
        x1 = x_ref[...] + mix
        acc_ref[...] = x1
        h_ref[...] = _rmsnorm(x1, nf_ref[...]).astype(BF16)

    @pl.when(i % tiles_per_seq == 0)
    def _():
        carry_ref[c] = cprev_ref[0]

    h = h_ref[...]
    u = jnp.dot(h, wu_ref[...], preferred_element_type=F32)
    gate = jnp.dot(h, wg_ref[...], preferred_element_type=F32)
    prev = carry_ref[c]
    p0, p1 = prev[0:1, :], prev[1:2, :]
    cw = cw_ref[...]
    uc = (cb_ref[...] + cw[0:1, :] * _shift_rows(u, 2, [p0, p1])
          + cw[1:2, :] * _shift_rows(u, 1, [p1]) + cw[2:3, :] * u)
    new_prev = u[tm - 2:tm, :]
    carry_ref[c] = new_prev
    cnew_ref[0, :, pl.ds(pl.multiple_of(c * fc, LANES), fc)] = new_prev
    act = 0.5 * uc * (1.0 + lax.erf(uc * (2.0 ** -0.5))) * gate
    acc_ref[...] += jnp.dot(act.astype(BF16), wd_ref[...], preferred_element_type=F32)

    @pl.when(c == n_fc - 1)
    def _():
        x2 = acc_ref[...]
        if final_norm:
            x2 = _rmsnorm(x2, nfin_ref[...])
        out_ref[...] = x2


def _post(x2d, o_gla, y_rwkv, prm, conv_prev, tm, seq_len, final_norm, fc):
    t = x2d.shape[0]
    b = conv_prev.shape[0]
    tiles_per_seq = seq_len // tm
    n_fc = D_FF // fc
    return pl.pallas_call(
        functools.partial(_post_kernel, tiles_per_seq=tiles_per_seq, n_fc=n_fc, fc=fc, tm=tm,
                          final_norm=final_norm),
        grid=(t // tm, n_fc),
        in_specs=[
            pl.BlockSpec((tm, D_MODEL), lambda i, c: (i, 0)),
            pl.BlockSpec((tm, GLA_V), lambda i, c: (i, 0)),
            pl.BlockSpec((tm, RWKV_C), lambda i, c: (i, 0)),
            pl.BlockSpec((D_MODEL, D_MODEL), lambda i, c: (0, 0)),
            pl.BlockSpec((1, D_MODEL), lambda i, c: (0, 0)),
            pl.BlockSpec((D_MODEL, fc), lambda i, c: (0, c)),
            pl.BlockSpec((D_MODEL, fc), lambda i, c: (0, c)),
            pl.BlockSpec((CONV_W, fc), lambda i, c: (0, c)),
            pl.BlockSpec((1, fc), lambda i, c: (0, c)),
            pl.BlockSpec((fc, D_MODEL), lambda i, c: (c, 0)),
            pl.BlockSpec((1, CONV_W - 1, fc), lambda i, c: (i // tiles_per_seq, 0, c)),
            pl.BlockSpec((1, D_MODEL), lambda i, c: (0, 0)),
        ],
        out_specs=[
            pl.BlockSpec((tm, D_MODEL), lambda i, c: (i, 0)),
            pl.BlockSpec((1, CONV_W - 1, D_FF), lambda i, c: (i // tiles_per_seq, 0, 0)),
        ],
        out_shape=[
            jax.ShapeDtypeStruct((t, D_MODEL), F32),
            jax.ShapeDtypeStruct((b, CONV_W - 1, D_FF), F32),
        ],
        scratch_shapes=[
            pltpu.VMEM((tm, D_MODEL), F32),
            pltpu.VMEM((tm, D_MODEL), BF16),
            pltpu.VMEM((n_fc, CONV_W - 1, fc), F32),
        ],
        compiler_params=pltpu.CompilerParams(
            dimension_semantics=("arbitrary", "arbitrary"), vmem_limit_bytes=VMEM_LIMIT_BYTES),
        name="post",
    )(x2d, o_gla, y_rwkv, prm["w_out"], prm["norm_ffn"], prm["w_up_u"], prm["w_up_g"],
      prm["conv_w"], prm["conv_b"], prm["w_down"], conv_prev, prm["norm_final"])


def _layer_params(l, P):
    w_in = P["w_in"][l]
    zpad = lambda a, rows_before, rows_total: jnp.zeros((rows_total, a.shape[1]), a.dtype).at[
        rows_before:rows_before + a.shape[0]].set(a)
    w_gla = jnp.concatenate(
        [w_in[:, :GLA_COLS], jnp.zeros((D_MODEL, GLA_SLAB - GLA_COLS), w_in.dtype)], axis=1)
    r = lambda a: a.reshape(1, -1)
    hid = jnp.arange(RWKV_C) // RWKV_N
    return dict(
        norm_mix=r(P["norm_mix"][l]),
        w_gla=w_gla.astype(BF16),
        w_rwkv=w_in[:, GLA_COLS:].astype(BF16),
        gate_w2p=zpad(P["gla_gate_w2"][l], 0, LANES).astype(BF16),
        gate_b=r(P["gla_gate_b"][l]),
        gla_norm_w=r(P["gla_norm_w"][l]),
        mu=r(P["rwkv_mu"][l]),
        w0=r(P["rwkv_w0"][l]),
        w2p=zpad(P["rwkv_w2"][l], 0, LANES).astype(BF16),
        a0=r(P["rwkv_a0"][l]),
        a2p=zpad(P["rwkv_a2"][l], RWKV_W_LR, LANES).astype(BF16),
        g2=P["rwkv_g2"][l].astype(BF16),
        k_k=r(P["rwkv_k_k"][l]), k_a=r(P["rwkv_k_a"][l]), r_k=r(P["rwkv_r_k"][l]),
        ln_w=r(P["rwkv_ln_w"][l]), ln_b=r(P["rwkv_ln_b"][l]),
        ones_bd=(hid[:, None] == hid[None, :]).astype(BF16),
        w_out=P["w_out"][l].astype(BF16),
        norm_ffn=r(P["norm_ffn"][l]),
        w_up_u=P["ffn_w_up"][l][:, :D_FF].astype(BF16),
        w_up_g=P["ffn_w_up"][l][:, D_FF:].astype(BF16),
        conv_w=P["ffn_conv_w"][l], conv_b=r(P["ffn_conv_b"][l]),
        w_down=P["ffn_w_down"][l].astype(BF16),
        norm_final=r(P["norm_final"]),
    )


def _gla_state_in(s):
    st = jnp.swapaxes(s, -1, -2)
    z = jnp.zeros_like(st)
    par = (jnp.arange(GLA_HEADS) % 2).reshape(1, GLA_HEADS, 1, 1)
    return jnp.where(par == 0, jnp.concatenate([st, z], -1), jnp.concatenate([z, st], -1))


def _gla_state_out(st):
    lo, hi = st[..., :GLA_DK], st[..., GLA_DK:]
    par = (jnp.arange(GLA_HEADS) % 2).reshape(1, GLA_HEADS, 1, 1)
    return jnp.swapaxes(jnp.where(par == 0, lo, hi), -1, -2)


def _rwkv_state_in(s):
    b = s.shape[0]
    s = s.reshape(b, RWKV_PAIRS, 2, RWKV_N, RWKV_N)
    z = jnp.zeros_like(s[:, :, 0])
    top = jnp.concatenate([s[:, :, 0], z], -1)
    bot = jnp.concatenate([z, s[:, :, 1]], -1)
    return jnp.concatenate([top, bot], -2)


def _rwkv_state_out(sb):
    b = sb.shape[0]
    s0 = sb[:, :, :RWKV_N, :RWKV_N]
    s1 = sb[:, :, RWKV_N:, RWKV_N:]
    return jnp.stack([s0, s1], axis=2).reshape(b, RWKV_HEADS, RWKV_N, RWKV_N)


def _tiles(seq_len):
    tm = min(512, seq_len)
    tl = min(256, seq_len)
    return tm, tl, D_FF // 2


def _trunk(x, s_gla, s_rwkv, s_shift, c_conv, layers):
    b, l, _ = x.shape
    tm, tl, fc = _tiles(l)
    x2d = x.reshape(b * l, D_MODEL)
    new_gla, new_rwkv, new_shift, new_conv = [], [], [], []
    for li, prm in enumerate(layers):
        p_gla, p_rwkv = _inproj(x2d, prm["norm_mix"], prm["w_gla"], prm["w_rwkv"], tm)
        o_gla, st = _gla(p_gla.reshape(b, l, GLA_SLAB), prm["gate_w2p"], prm["gate_b"],
                         prm["gla_norm_w"], _gla_state_in(s_gla[li]), tl)
        y_rwkv, sb, sh = _rwkv(p_rwkv.reshape(b, l, RWKV_COLS), s_shift[li][:, None, :], prm,
                               _rwkv_state_in(s_rwkv[li]), tl)
        x2d, cc = _post(x2d, o_gla.reshape(b * l, GLA_V), y_rwkv.reshape(b * l, RWKV_C), prm,
                        c_conv[li], tm, l, li == len(layers) - 1, fc)
        new_gla.append(_gla_state_out(st))
        new_rwkv.append(_rwkv_state_out(sb))
        new_shift.append(sh[:, 0, :])
        new_conv.append(cc)
    return (x2d.reshape(b, l, D_MODEL), jnp.stack(new_gla), jnp.stack(new_rwkv),
            jnp.stack(new_shift), jnp.stack(new_conv))


def kernel(x_prompt, x_sample, state_gla, state_rwkv, state_rwkv_shift, cache_ffn_conv, norm_mix, w_in, gla_gate_w2, gla_gate_b, gla_norm_w, rwkv_mu, rwkv_w0, rwkv_w2, rwkv_a0, rwkv_a2, rwkv_g2, rwkv_k_k, rwkv_k_a, rwkv_r_k, rwkv_ln_w, rwkv_ln_b, w_out, norm_ffn, ffn_w_up, ffn_conv_w, ffn_conv_b, ffn_w_down, norm_final):
    P = dict(norm_mix=norm_mix, w_in=w_in, gla_gate_w2=gla_gate_w2, gla_gate_b=gla_gate_b,
             gla_norm_w=gla_norm_w, rwkv_mu=rwkv_mu, rwkv_w0=rwkv_w0, rwkv_w2=rwkv_w2,
             rwkv_a0=rwkv_a0, rwkv_a2=rwkv_a2, rwkv_g2=rwkv_g2, rwkv_k_k=rwkv_k_k,
             rwkv_k_a=rwkv_k_a, rwkv_r_k=rwkv_r_k.reshape(DEPTH, RWKV_C), rwkv_ln_w=rwkv_ln_w,
             rwkv_ln_b=rwkv_ln_b, w_out=w_out, norm_ffn=norm_ffn, ffn_w_up=ffn_w_up,
             ffn_conv_w=ffn_conv_w, ffn_conv_b=ffn_conv_b, ffn_w_down=ffn_w_down,
             norm_final=norm_final)
    layers = [_layer_params(l, P) for l in range(DEPTH)]
    nb = x_prompt.shape[0]
    g0 = jnp.zeros((DEPTH, nb) + state_gla.shape[2:], state_gla.dtype)
    r0 = jnp.zeros((DEPTH, nb) + state_rwkv.shape[2:], state_rwkv.dtype)
    sh0 = jnp.zeros((DEPTH, nb) + state_rwkv_shift.shape[2:], state_rwkv_shift.dtype)
    c0 = jnp.zeros((DEPTH, nb) + cache_ffn_conv.shape[2:], cache_ffn_conv.dtype)
    y_p, gla_p, rwkv_p, shift_p, conv_p = _trunk(x_prompt, g0, r0, sh0, c0, layers)
    y_s, gla_s, rwkv_s, shift_s, conv_s = _trunk(
        x_sample, state_gla, state_rwkv, state_rwkv_shift, cache_ffn_conv, layers)
    return (y_p, y_s, gla_p, rwkv_p, shift_p, conv_p, gla_s, rwkv_s, shift_s, conv_s)
```

```python
import functools

import jax
import jax.numpy as jnp
from jax import lax
from jax.experimental import pallas as pl
from jax.experimental.pallas import tpu as pltpu

F32 = jnp.float32
BF16 = jnp.bfloat16

D_MODEL = 1024
DEPTH = 2
CHUNK = 64
NORM_EPS = 1e-6
GLA_HEADS = 4
GLA_DV = 128
GLA_DK = 64
GLA_K = GLA_HEADS * GLA_DK
GLA_V = GLA_HEADS * GLA_DV
GLA_LR = 16
GLA_GATE_NORMALIZER = 16.0
RWKV_N = 64
RWKV_C = 512
RWKV_HEADS = 8
RWKV_PAIRS = RWKV_HEADS // 2
RWKV_W_LR = 64
RWKV_A_LR = 64
RWKV_G_LR = 128
RWKV_GN_EPS = 64e-5
GLA_COLS = 2 * GLA_K + 2 * GLA_V + GLA_LR
RWKV_COLS = 3 * RWKV_C + RWKV_W_LR + RWKV_A_LR + RWKV_G_LR
D_FF = 2816
CONV_W = 3

LANES = 128
MXU_TILE = 256
GLA_SLAB = 2 * GLA_K + 2 * GLA_V + LANES
VMEM_LIMIT_BYTES = 56 * 1024 * 1024

_GQ, _GK, _GV, _GG, _GLR = 0, GLA_K, 2 * GLA_K, 2 * GLA_K + GLA_V, 2 * GLA_K + 2 * GLA_V
_RR, _RK, _RV, _RWA, _RG = 0, RWKV_C, 2 * RWKV_C, 3 * RWKV_C, 3 * RWKV_C + RWKV_W_LR + RWKV_A_LR


def _bdot(a, b):
    return jnp.dot(a.astype(BF16), b.astype(BF16), preferred_element_type=F32)


def _bdot_nt(a, b):
    return lax.dot_general(a.astype(BF16), b.astype(BF16), (((1,), (1,)), ((), ())),
                           preferred_element_type=F32)


def _bdot_tn(a, b):
    return lax.dot_general(a.astype(BF16), b.astype(BF16), (((0,), (0,)), ((), ())),
                           preferred_element_type=F32)


def _split3(x):
    hi = x.astype(BF16)
    r1 = x - hi.astype(F32)
    mid = r1.astype(BF16)
    lo = (r1 - mid.astype(F32)).astype(BF16)
    return hi, mid, lo


def _exact_dot(m_bf16, x):
    hi, mid, lo = _split3(x)
    d = lambda p: jnp.dot(m_bf16, p, preferred_element_type=F32)
    return d(hi) + d(mid) + d(lo)


def _head_sum(x):
    r = lax.broadcasted_iota(jnp.int32, (MXU_TILE, MXU_TILE), 0)
    c = lax.broadcasted_iota(jnp.int32, (MXU_TILE, MXU_TILE), 1)
    ones_blk = ((r // RWKV_N) == (c // RWKV_N)).astype(BF16)
    hi = x.astype(BF16)
    lo = (x - hi.astype(F32)).astype(BF16)
    out = []
    for g in range(x.shape[1] // MXU_TILE):
        cols = slice(g * MXU_TILE, (g + 1) * MXU_TILE)
        out.append(jnp.dot(hi[:, cols], ones_blk, preferred_element_type=F32)
                   + jnp.dot(lo[:, cols], ones_blk, preferred_element_type=F32))
    return jnp.concatenate(out, axis=1)


def _tril_incl(n):
    r = lax.broadcasted_iota(jnp.int32, (n, n), 0)
    c = lax.broadcasted_iota(jnp.int32, (n, n), 1)
    return (c <= r)


def _rmsnorm(x, g):
    return x * lax.rsqrt(jnp.mean(x * x, axis=-1, keepdims=True) + NORM_EPS) * g


def _shift_rows(x, k, fill_rows):
    y = pltpu.roll(x, k, 0)
    row = lax.broadcasted_iota(jnp.int32, x.shape, 0)
    for i, f in enumerate(fill_rows):
        y = jnp.where(row == i, f, y)
    return y


def _inproj_kernel(x_ref, g_ref, wg_ref, wr_ref, og_ref, or_ref):
    h = _rmsnorm(x_ref[...], g_ref[...]).astype(BF16)
    og_ref[...] = jnp.dot(h, wg_ref[...], preferred_element_type=F32)
    or_ref[...] = jnp.dot(h, wr_ref[...], preferred_element_type=F32)


def _inproj(x2d, g, w_gla, w_rwkv, tm):
    t = x2d.shape[0]
    return pl.pallas_call(
        _inproj_kernel,
        grid=(t // tm,),
        in_specs=[
            pl.BlockSpec((tm, D_MODEL), lambda i: (i, 0)),
            pl.BlockSpec((1, D_MODEL), lambda i: (0, 0)),
            pl.BlockSpec((D_MODEL, GLA_SLAB), lambda i: (0, 0)),
            pl.BlockSpec((D_MODEL, RWKV_COLS), lambda i: (0, 0)),
        ],
        out_specs=[
            pl.BlockSpec((tm, GLA_SLAB), lambda i: (i, 0)),
            pl.BlockSpec((tm, RWKV_COLS), lambda i: (i, 0)),
        ],
        out_shape=[
            jax.ShapeDtypeStruct((t, GLA_SLAB), F32),
            jax.ShapeDtypeStruct((t, RWKV_COLS), F32),
        ],
        compiler_params=pltpu.CompilerParams(
            dimension_semantics=("arbitrary",), vmem_limit_bytes=VMEM_LIMIT_BYTES),
        name="inproj",
    )(x2d, g, w_gla, w_rwkv)


def _gla_kernel(p_ref, w2_ref, b_ref, nw_ref, s0_ref, o_ref, sout_ref, st_ref, *, n_chunks):
    ti = pl.program_id(1)

    @pl.when(ti == 0)
    def _():
        st_ref[...] = s0_ref[0]

    tl = n_chunks * CHUNK
    ri = lax.broadcasted_iota(jnp.int32, (tl, tl), 0)
    ci = lax.broadcasted_iota(jnp.int32, (tl, tl), 1)
    same_chunk = (ri // CHUNK) == (ci // CHUNK)
    causal = same_chunk & (ci <= ri)
    lane = lax.broadcasted_iota(jnp.int32, (tl, LANES), 1)
    half = [lane < GLA_DK, lane >= GLA_DK]
    nw = nw_ref[...]

    logit = _bdot(p_ref[0, :, _GLR:_GLR + LANES], w2_ref[...]) + b_ref[...]
    la = jax.nn.log_sigmoid(logit) / GLA_GATE_NORMALIZER
    cum = _exact_dot(causal.astype(BF16), la)
    tot = _exact_dot(same_chunk.astype(BF16), la)
    q = p_ref[0, :, _GQ:_GQ + GLA_K] * (GLA_DK ** -0.5)
    k = p_ref[0, :, _GK:_GK + GLA_K]
    q_e = q * jnp.exp(cum)
    k_e = k * jnp.exp(-cum)
    k_hat = k * jnp.exp(tot - cum)
    dec = jnp.exp(tot)
    heads = range(GLA_HEADS)
    chunks = range(n_chunks)
    crow = lambda x, c: x[c * CHUNK:(c + 1) * CHUNK]
    pair = lambda x, h: x[:, (h // 2) * LANES:(h // 2 + 1) * LANES]
    v = [p_ref[0, :, _GV + h * GLA_DV:_GV + (h + 1) * GLA_DV] for h in heads]

    sc = [_bdot_nt(jnp.where(half[h % 2], pair(q_e, h), 0.0), pair(k_e, h)) for h in heads]
    o_intra = [_bdot(jnp.where(causal, sc[h], 0.0), v[h]) for h in heads]

    kh_m = [jnp.where(half[h % 2], pair(k_hat, h), 0.0) for h in heads]
    upd = [[_bdot_tn(crow(v[h], c), crow(kh_m[h], c)) for c in chunks] for h in heads]
    st_in = []
    for h in heads:
        st = st_ref[h]
        per_chunk = []
        for c in chunks:
            per_chunk.append(st)
            st = st * pair(dec, h)[c * CHUNK:c * CHUNK + 1] + upd[h][c]
        st_ref[h] = st
        st_in.append(per_chunk)
    o_inter = [jnp.concatenate([_bdot_nt(crow(pair(q_e, h), c), st_in[h][c]) for c in chunks], axis=0)
               for h in heads]

    for h in heads:
        o = o_intra[h] + o_inter[h]
        gg = p_ref[0, :, _GG + h * GLA_DV:_GG + (h + 1) * GLA_DV]
        o = o * lax.rsqrt(jnp.mean(o * o, axis=-1, keepdims=True) + NORM_EPS) * nw
        o = o * (gg * jax.nn.sigmoid(gg))
        o_ref[0, :, h * GLA_DV:(h + 1) * GLA_DV] = o.astype(o_ref.dtype)
    sout_ref[0] = st_ref[...]


def _gla(p_gla, w2p, gate_b, norm_w, st0, tl):
    b, l, _ = p_gla.shape
    return pl.pallas_call(
        functools.partial(_gla_kernel, n_chunks=tl // CHUNK),
        grid=(b, l // tl),
        in_specs=[
            pl.BlockSpec((1, tl, GLA_SLAB), lambda i, j: (i, j, 0)),
            pl.BlockSpec((LANES, GLA_K), lambda i, j: (0, 0)),
            pl.BlockSpec((1, GLA_K), lambda i, j: (0, 0)),
            pl.BlockSpec((1, GLA_DV), lambda i, j: (0, 0)),
            pl.BlockSpec((1, GLA_HEADS, GLA_DV, LANES), lambda i, j: (i, 0, 0, 0)),
        ],
        out_specs=[
            pl.BlockSpec((1, tl, GLA_V), lambda i, j: (i, j, 0)),
            pl.BlockSpec((1, GLA_HEADS, GLA_DV, LANES), lambda i, j: (i, 0, 0, 0)),
        ],
        out_shape=[
            jax.ShapeDtypeStruct((b, l, GLA_V), BF16),
            jax.ShapeDtypeStruct((b, GLA_HEADS, GLA_DV, LANES), F32),
        ],
        scratch_shapes=[pltpu.VMEM((GLA_HEADS, GLA_DV, LANES), F32)],
        compiler_params=pltpu.CompilerParams(
            dimension_semantics=("arbitrary", "arbitrary"), vmem_limit_bytes=VMEM_LIMIT_BYTES),
        name="gla",
    )(p_gla, w2p, gate_b, norm_w, st0)


def _rwkv_kernel(p_ref, sh_ref, mu_ref, w0_ref, w2_ref, a0_ref, a2_ref, g2_ref, kk_ref, ka_ref,
                 rk_ref, lnw_ref, lnb_ref, s0_ref,
                 y_ref, sout_ref, shout_ref,
                 s_ref, prev_ref, r_s, w_s, k_s, v_s, a_s, b_s, y_s, *, n_chunks, group, tl):
    ti = pl.program_id(1)

    @pl.when(ti == 0)
    def _():
        s_ref[...] = s0_ref[0]
        prev_ref[...] = sh_ref[0]

    seg_sum = _head_sum

    p = p_ref[0]
    prev = _shift_rows(p, 1, [prev_ref[...]])
    prev_ref[...] = p[tl - 1:tl, :]
    shout_ref[0] = p[tl - 1:tl, :]
    xs = p + mu_ref[...] * (prev - p)
    xr = xs[:, _RR:_RR + RWKV_C]
    xk = xs[:, _RK:_RK + RWKV_C]
    xv = xs[:, _RV:_RV + RWKV_C]
    xwa = xs[:, _RWA:_RWA + LANES]
    xg = xs[:, _RG:_RG + RWKV_G_LR]
    w_raw = w0_ref[...] + _bdot(jnp.tanh(xwa), w2_ref[...])
    w_log = -jnp.exp(-jax.nn.softplus(-w_raw) - 0.5)
    a = jax.nn.sigmoid(a0_ref[...] + _bdot(xwa, a2_ref[...]))
    g = _bdot(jax.nn.sigmoid(xg), g2_ref[...])
    kk = xk * kk_ref[...]
    kk = kk / jnp.maximum(jnp.sqrt(seg_sum(kk * kk)), 1e-12)
    kr = xk * (1.0 + (a - 1.0) * ka_ref[...])
    bonus = seg_sum(xr * kr * rk_ref[...]) * xv
    r_s[...] = xr
    w_s[...] = w_log
    k_s[...] = kr
    v_s[...] = xv
    a_s[...] = -kk
    b_s[...] = kk * a

    tril_bf = _tril_incl(CHUNK).astype(BF16)
    lane = lax.broadcasted_iota(jnp.int32, (CHUNK, LANES), 1)
    lo_half = lane < RWKV_N
    ri = lax.broadcasted_iota(jnp.int32, (LANES, LANES), 0)
    ci = lax.broadcasted_iota(jnp.int32, (LANES, LANES), 1)
    strict = (ci & (RWKV_N - 1)) < (ri & (RWKV_N - 1))
    incl = (ci & (RWKV_N - 1)) <= (ri & (RWKV_N - 1))
    same_head = (ri >> 6) == (ci >> 6)
    eye = (ri == ci).astype(F32)

    def stack(x):
        return jnp.concatenate([jnp.where(lo_half, x, 0.0), jnp.where(lo_half, 0.0, x)], axis=0)

    def tile(x):
        return jnp.concatenate([x, x], axis=0)

    def fold(x):
        return jnp.where(lo_half, x[:CHUNK], x[CHUNK:])

    def body(gi, carry):
        units = []
        for cc in range(group):
            rows = pl.ds(pl.multiple_of((gi * group + cc) * CHUNK, CHUNK), CHUNK)
            lw_all = w_s[rows, :]
            cl_all = _exact_dot(tril_bf, lw_all)
            for pr in range(RWKV_PAIRS):
                units.append((rows, slice(pr * LANES, (pr + 1) * LANES), pr, lw_all, cl_all))
        n = range(len(units))
        a_t, r_t, b_t, k_t, b_h, k_h, v, dec = [], [], [], [], [], [], [], []
        for rows, cols, _, lw_all, cl_all in units:
            lw, cl = lw_all[:, cols], cl_all[:, cols]
            cl_end = cl[CHUNK - 1:CHUNK, :]
            e_neg = jnp.exp(-cl)
            e_end = jnp.exp(cl_end - cl)
            av, bv, kv = a_s[rows, cols], b_s[rows, cols], k_s[rows, cols]
            a_t.append(av * jnp.exp(cl - lw))
            r_t.append(r_s[rows, cols] * jnp.exp(cl))
            b_t.append(bv * e_neg)
            k_t.append(kv * e_neg)
            b_h.append(bv * e_end)
            k_h.append(kv * e_end)
            v.append(v_s[rows, cols])
            dec.append(jnp.exp(cl_end))
        gm = [_bdot_nt(jnp.concatenate([stack(a_t[i]), stack(r_t[i])], axis=0),
                       jnp.concatenate([stack(b_t[i]), stack(k_t[i])], axis=0)) for i in n]
        a_ab = [jnp.where(strict, gm[i][:LANES, :LANES], 0.0) for i in n]
        a_ak = [jnp.where(strict, gm[i][:LANES, LANES:], 0.0) for i in n]
        r_b = [jnp.where(incl, gm[i][LANES:, :LANES], 0.0) for i in n]
        r_k = [jnp.where(incl, gm[i][LANES:, LANES:], 0.0) for i in n]
        inv = [eye + a_ab[i] for i in n]
        pw = a_ab
        for _ in range(5):
            pw = [_bdot(pw[i], pw[i]) for i in n]
            inv = [inv[i] + _bdot(pw[i], inv[i]) for i in n]
        akv = [fold(_bdot(a_ak[i], tile(v[i]))) for i in n]
        w = [_bdot(inv[i], jnp.concatenate([stack(a_t[i]), tile(akv[i])], axis=1)) for i in n]
        a_bar = [fold(w[i][:, :LANES]) for i in n]
        u0 = [fold(w[i][:, LANES:]) for i in n]
        x = [_bdot(r_b[i], jnp.concatenate([stack(a_bar[i]), tile(u0[i])], axis=1)) for i in n]
        rkv = [_bdot(r_k[i], tile(v[i])) for i in n]
        r_bar = [r_t[i] + fold(x[i][:, :LANES]) for i in n]
        y0 = [fold(x[i][:, LANES:]) + fold(rkv[i]) for i in n]
        m1 = [jnp.where(same_head, _bdot_tn(a_bar[i], b_h[i]), 0.0) for i in n]
        z = [jnp.where(same_head, _bdot_tn(jnp.concatenate([u0[i], v[i]], axis=0),
                                           jnp.concatenate([b_h[i], k_h[i]], axis=0)), 0.0) for i in n]
        s = [s_ref[pr] for pr in range(RWKV_PAIRS)]
        for i in n:
            rows, cols, pr = units[i][:3]
            y_s[rows, cols] = _bdot_nt(r_bar[i], s[pr]) + y0[i]
            s[pr] = s[pr] * dec[i] + _bdot(s[pr], m1[i]) + z[i]
        for pr in range(RWKV_PAIRS):
            s_ref[pr] = s[pr]
        return carry

    lax.fori_loop(0, n_chunks // group, body, 0)
    sout_ref[0] = s_ref[...]

    y = y_s[...]
    mean = seg_sum(y) * (1.0 / RWKV_N)
    yc = y - mean
    var = seg_sum(yc * yc) * (1.0 / RWKV_N)
    y = yc * lax.rsqrt(var + RWKV_GN_EPS) * lnw_ref[...] + lnb_ref[...]
    y_ref[0] = ((y + bonus) * g).astype(y_ref.dtype)


def _rwkv(p_rwkv, shift_prev, prm, s0, tl):
    b, l, _ = p_rwkv.shape
    row = lambda n: pl.BlockSpec((1, n), lambda i, j: (0, 0))
    mat = lambda m, n: pl.BlockSpec((m, n), lambda i, j: (0, 0))
    return pl.pallas_call(
        functools.partial(_rwkv_kernel, n_chunks=tl // CHUNK, group=min(2, tl // CHUNK), tl=tl),
        grid=(b, l // tl),
        in_specs=[
            pl.BlockSpec((1, tl, RWKV_COLS), lambda i, j: (i, j, 0)),
            pl.BlockSpec((1, 1, RWKV_COLS), lambda i, j: (i, 0, 0)),
            row(RWKV_COLS),
            row(RWKV_C), mat(LANES, RWKV_C), row(RWKV_C), mat(LANES, RWKV_C), mat(RWKV_G_LR, RWKV_C),
            row(RWKV_C), row(RWKV_C), row(RWKV_C), row(RWKV_C), row(RWKV_C),
            pl.BlockSpec((1, RWKV_PAIRS, LANES, LANES), lambda i, j: (i, 0, 0, 0)),
        ],
        out_specs=[
            pl.BlockSpec((1, tl, RWKV_C), lambda i, j: (i, j, 0)),
            pl.BlockSpec((1, RWKV_PAIRS, LANES, LANES), lambda i, j: (i, 0, 0, 0)),
            pl.BlockSpec((1, 1, RWKV_COLS), lambda i, j: (i, 0, 0)),
        ],
        out_shape=[
            jax.ShapeDtypeStruct((b, l, RWKV_C), BF16),
            jax.ShapeDtypeStruct((b, RWKV_PAIRS, LANES, LANES), F32),
            jax.ShapeDtypeStruct((b, 1, RWKV_COLS), F32),
        ],
        scratch_shapes=[
            pltpu.VMEM((RWKV_PAIRS, LANES, LANES), F32),
            pltpu.VMEM((1, RWKV_COLS), F32),
        ] + [pltpu.VMEM((tl, RWKV_C), F32)] * 7,
        compiler_params=pltpu.CompilerParams(
            dimension_semantics=("arbitrary", "arbitrary"), vmem_limit_bytes=VMEM_LIMIT_BYTES),
        name="rwkv",
    )(p_rwkv, shift_prev, prm["mu"], prm["w0"], prm["w2p"], prm["a0"], prm["a2p"], prm["g2"],
      prm["k_k"], prm["k_a"], prm["r_k"], prm["ln_w"], prm["ln_b"], s0)


def _post_kernel(x_ref, og_ref, yr_ref, wo_ref, nf_ref, wu_ref, wg_ref, cw_ref, cb_ref, wd_ref,
                 cprev_ref, nfin_ref, out_ref, cnew_ref, acc_ref, h_ref, carry_ref,
                 *, tiles_per_seq, n_fc, fc, tm, final_norm):
    i = pl.program_id(0)
    c = pl.program_id(1)

    @pl.when(c == 0)
    def _():
        wo = wo_ref[...]
        mix = (jnp.dot(og_ref[...], wo[:GLA_V], preferred_element_type=F32)
               + jnp.dot(yr_ref[...], wo[GLA_V:], preferred_element_type=F32))
        x1 = x_ref[...] + mix
        acc_ref[...] = x1
        h_ref[...] = _rmsnorm(x1, nf_ref[...]).astype(BF16)

    @pl.when(i % tiles_per_seq == 0)
    def _():
        carry_ref[c] = cprev_ref[0]

    h = h_ref[...]
    u = jnp.dot(h, wu_ref[...], preferred_element_type=F32)
    gate = jnp.dot(h, wg_ref[...], preferred_element_type=F32)
    prev = carry_ref[c]
    p0, p1 = prev[0:1, :], prev[1:2, :]
    cw = cw_ref[...]
    uc = (cb_ref[...] + cw[0:1, :] * _shift_rows(u, 2, [p0, p1])
          + cw[1:2, :] * _shift_rows(u, 1, [p1]) + cw[2:3, :] * u)
    new_prev = u[tm - 2:tm, :]
    carry_ref[c] = new_prev
    cnew_ref[0, :, pl.ds(pl.multiple_of(c * fc, LANES), fc)] = new_prev
    act = 0.5 * uc * (1.0 + lax.erf(uc * (2.0 ** -0.5))) * gate
    acc_ref[...] += jnp.dot(act.astype(BF16), wd_ref[...], preferred_element_type=F32)

    @pl.when(c == n_fc - 1)
    def _():
        x2 = acc_ref[...]
        if final_norm:
            x2 = _rmsnorm(x2, nfin_ref[...])
        out_ref[...] = x2


def _post(x2d, o_gla, y_rwkv, prm, conv_prev, tm, seq_len, final_norm, fc):
    t = x2d.shape[0]
    b = conv_prev.shape[0]
    tiles_per_seq = seq_len // tm
    n_fc = D_FF // fc
    return pl.pallas_call(
        functools.partial(_post_kernel, tiles_per_seq=tiles_per_seq, n_fc=n_fc, fc=fc, tm=tm,
                          final_norm=final_norm),
        grid=(t // tm, n_fc),
        in_specs=[
            pl.BlockSpec((tm, D_MODEL), lambda i, c: (i, 0)),
            pl.BlockSpec((tm, GLA_V), lambda i, c: (i, 0)),
            pl.BlockSpec((tm, RWKV_C), lambda i, c: (i, 0)),
            pl.BlockSpec((D_MODEL, D_MODEL), lambda i, c: (0, 0)),
            pl.BlockSpec((1, D_MODEL), lambda i, c: (0, 0)),
            pl.BlockSpec((D_MODEL, fc), lambda i, c: (0, c)),
            pl.BlockSpec((D_MODEL, fc), lambda i, c: (0, c)),
            pl.BlockSpec((CONV_W, fc), lambda i, c: (0, c)),
            pl.BlockSpec((1, fc), lambda i, c: (0, c)),
            pl.BlockSpec((fc, D_MODEL), lambda i, c: (c, 0)),
            pl.BlockSpec((1, CONV_W - 1, fc), lambda i, c: (i // tiles_per_seq, 0, c)),
            pl.BlockSpec((1, D_MODEL), lambda i, c: (0, 0)),
        ],
        out_specs=[
            pl.BlockSpec((tm, D_MODEL), lambda i, c: (i, 0)),
            pl.BlockSpec((1, CONV_W - 1, D_FF), lambda i, c: (i // tiles_per_seq, 0, 0)),
        ],
        out_shape=[
            jax.ShapeDtypeStruct((t, D_MODEL), F32),
            jax.ShapeDtypeStruct((b, CONV_W - 1, D_FF), F32),
        ],
        scratch_shapes=[
            pltpu.VMEM((tm, D_MODEL), F32),
            pltpu.VMEM((tm, D_MODEL), BF16),
            pltpu.VMEM((n_fc, CONV_W - 1, fc), F32),
        ],
        compiler_params=pltpu.CompilerParams(
            dimension_semantics=("arbitrary", "arbitrary"), vmem_limit_bytes=VMEM_LIMIT_BYTES),
        name="post",
    )(x2d, o_gla, y_rwkv, prm["w_out"], prm["norm_ffn"], prm["w_up_u"], prm["w_up_g"],
      prm["conv_w"], prm["conv_b"], prm["w_down"], conv_prev, prm["norm_final"])


def _layer_params(l, P):
    w_in = P["w_in"][l]
    zpad = lambda a, rows_before, rows_total: jnp.zeros((rows_total, a.shape[1]), a.dtype).at[
        rows_before:rows_before + a.shape[0]].set(a)
    w_gla = jnp.concatenate(
        [w_in[:, :GLA_COLS], jnp.zeros((D_MODEL, GLA_SLAB - GLA_COLS), w_in.dtype)], axis=1)
    r = lambda a: a.reshape(1, -1)
    return dict(
        norm_mix=r(P["norm_mix"][l]),
        w_gla=w_gla.astype(BF16),
        w_rwkv=w_in[:, GLA_COLS:].astype(BF16),
        gate_w2p=zpad(P["gla_gate_w2"][l], 0, LANES).astype(BF16),
        gate_b=r(P["gla_gate_b"][l]),
        gla_norm_w=r(P["gla_norm_w"][l]),
        mu=r(P["rwkv_mu"][l]),
        w0=r(P["rwkv_w0"][l]),
        w2p=zpad(P["rwkv_w2"][l], 0, LANES).astype(BF16),
        a0=r(P["rwkv_a0"][l]),
        a2p=zpad(P["rwkv_a2"][l], RWKV_W_LR, LANES).astype(BF16),
        g2=P["rwkv_g2"][l].astype(BF16),
        k_k=r(P["rwkv_k_k"][l]), k_a=r(P["rwkv_k_a"][l]), r_k=r(P["rwkv_r_k"][l]),
        ln_w=r(P["rwkv_ln_w"][l]), ln_b=r(P["rwkv_ln_b"][l]),
        w_out=P["w_out"][l].astype(BF16),
        norm_ffn=r(P["norm_ffn"][l]),
        w_up_u=P["ffn_w_up"][l][:, :D_FF].astype(BF16),
        w_up_g=P["ffn_w_up"][l][:, D_FF:].astype(BF16),
        conv_w=P["ffn_conv_w"][l], conv_b=r(P["ffn_conv_b"][l]),
        w_down=P["ffn_w_down"][l].astype(BF16),
        norm_final=r(P["norm_final"]),
    )


def _gla_state_in(s):
    st = jnp.swapaxes(s, -1, -2)
    z = jnp.zeros_like(st)
    par = (jnp.arange(GLA_HEADS) % 2).reshape(1, GLA_HEADS, 1, 1)
    return jnp.where(par == 0, jnp.concatenate([st, z], -1), jnp.concatenate([z, st], -1))


def _gla_state_out(st):
    lo, hi = st[..., :GLA_DK], st[..., GLA_DK:]
    par = (jnp.arange(GLA_HEADS) % 2).reshape(1, GLA_HEADS, 1, 1)
    return jnp.swapaxes(jnp.where(par == 0, lo, hi), -1, -2)


def _rwkv_state_in(s):
    b = s.shape[0]
    s = s.reshape(b, RWKV_PAIRS, 2, RWKV_N, RWKV_N)
    z = jnp.zeros_like(s[:, :, 0])
    top = jnp.concatenate([s[:, :, 0], z], -1)
    bot = jnp.concatenate([z, s[:, :, 1]], -1)
    return jnp.concatenate([top, bot], -2)


def _rwkv_state_out(sb):
    b = sb.shape[0]
    s0 = sb[:, :, :RWKV_N, :RWKV_N]
    s1 = sb[:, :, RWKV_N:, RWKV_N:]
    return jnp.stack([s0, s1], axis=2).reshape(b, RWKV_HEADS, RWKV_N, RWKV_N)


def _tiles(seq_len):
    tm = min(512, seq_len)
    tl = min(256, seq_len)
    return tm, tl, D_FF // 2


def _trunk(x, s_gla, s_rwkv, s_shift, c_conv, layers):
    b, l, _ = x.shape
    tm, tl, fc = _tiles(l)
    x2d = x.reshape(b * l, D_MODEL)
    new_gla, new_rwkv, new_shift, new_conv = [], [], [], []
    for li, prm in enumerate(layers):
        p_gla, p_rwkv = _inproj(x2d, prm["norm_mix"], prm["w_gla"], prm["w_rwkv"], tm)
        o_gla, st = _gla(p_gla.reshape(b, l, GLA_SLAB), prm["gate_w2p"], prm["gate_b"],
                         prm["gla_norm_w"], _gla_state_in(s_gla[li]), tl)
        y_rwkv, sb, sh = _rwkv(p_rwkv.reshape(b, l, RWKV_COLS), s_shift[li][:, None, :], prm,
                               _rwkv_state_in(s_rwkv[li]), tl)
        x2d, cc = _post(x2d, o_gla.reshape(b * l, GLA_V), y_rwkv.reshape(b * l, RWKV_C), prm,
                        c_conv[li], tm, l, li == len(layers) - 1, fc)
        new_gla.append(_gla_state_out(st))
        new_rwkv.append(_rwkv_state_out(sb))
        new_shift.append(sh[:, 0, :])
        new_conv.append(cc)
    return (x2d.reshape(b, l, D_MODEL), jnp.stack(new_gla), jnp.stack(new_rwkv),
            jnp.stack(new_shift), jnp.stack(new_conv))


def kernel(x_prompt, x_sample, state_gla, state_rwkv, state_rwkv_shift, cache_ffn_conv, norm_mix, w_in, gla_gate_w2, gla_gate_b, gla_norm_w, rwkv_mu, rwkv_w0, rwkv_w2, rwkv_a0, rwkv_a2, rwkv_g2, rwkv_k_k, rwkv_k_a, rwkv_r_k, rwkv_ln_w, rwkv_ln_b, w_out, norm_ffn, ffn_w_up, ffn_conv_w, ffn_conv_b, ffn_w_down, norm_final):
    P = dict(norm_mix=norm_mix, w_in=w_in, gla_gate_w2=gla_gate_w2, gla_gate_b=gla_gate_b,
             gla_norm_w=gla_norm_w, rwkv_mu=rwkv_mu, rwkv_w0=rwkv_w0, rwkv_w2=rwkv_w2,
             rwkv_a0=rwkv_a0, rwkv_a2=rwkv_a2, rwkv_g2=rwkv_g2, rwkv_k_k=rwkv_k_k,
             rwkv_k_a=rwkv_k_a, rwkv_r_k=rwkv_r_k.reshape(DEPTH, RWKV_C), rwkv_ln_w=rwkv_ln_w,
             rwkv_ln_b=rwkv_ln_b, w_out=w_out, norm_ffn=norm_ffn, ffn_w_up=ffn_w_up,
             ffn_conv_w=ffn_conv_w, ffn_conv_b=ffn_conv_b, ffn_w_down=ffn_w_down,
             norm_final=norm_final)
    layers = [_layer_params(l, P) for l in range(DEPTH)]
    nb = x_prompt.shape[0]
    g0 = jnp.zeros((DEPTH, nb) + state_gla.shape[2:], state_gla.dtype)
    r0 = jnp.zeros((DEPTH, nb) + state_rwkv.shape[2:], state_rwkv.dtype)
    sh0 = jnp.zeros((DEPTH, nb) + state_rwkv_shift.shape[2:], state_rwkv_shift.dtype)
    c0 = jnp.zeros((DEPTH, nb) + cache_ffn_conv.shape[2:], cache_ffn_conv.dtype)
    y_p, gla_p, rwkv_p, shift_p, conv_p = _trunk(x_prompt, g0, r0, sh0, c0, layers)
    y_s, gla_s, rwkv_s, shift_s, conv_s = _trunk(
        x_sample, state_gla, state_rwkv, state_rwkv_shift, cache_ffn_conv, layers)
    return (y_p, y_s, gla_p, rwkv_p, shift_p, conv_p, gla_s, rwkv_s, shift_s, conv_s)
```

```python
import functools
import math

import jax
import jax.numpy as jnp
from jax import lax
from jax.experimental import pallas as pl
from jax.experimental.pallas import tpu as pltpu

F32 = jnp.float32
BF16 = jnp.bfloat16

D_MODEL = 1024
DEPTH = 2
CHUNK = 64
NORM_EPS = 1e-6
GLA_HEADS = 4
GLA_DV = 128
GLA_DK = 64
GLA_K = GLA_HEADS * GLA_DK
GLA_V = GLA_HEADS * GLA_DV
GLA_LR = 16
GLA_GATE_NORMALIZER = 16.0
RWKV_N = 64
RWKV_C = 512
RWKV_HEADS = 8
RWKV_PAIRS = RWKV_HEADS // 2
RWKV_W_LR = 64
RWKV_A_LR = 64
RWKV_G_LR = 128
RWKV_GN_EPS = 64e-5
GLA_COLS = 2 * GLA_K + 2 * GLA_V + GLA_LR
RWKV_COLS = 3 * RWKV_C + RWKV_W_LR + RWKV_A_LR + RWKV_G_LR
D_FF = 2816
CONV_W = 3

LANES = 128
MXU_TILE = 256
GLA_SLAB = 2 * GLA_K + 2 * GLA_V + LANES
VMEM_LIMIT_BYTES = 56 * 1024 * 1024

_GQ, _GK, _GV, _GG, _GLR = 0, GLA_K, 2 * GLA_K, 2 * GLA_K + GLA_V, 2 * GLA_K + 2 * GLA_V
_RR, _RK, _RV, _RWA, _RG = 0, RWKV_C, 2 * RWKV_C, 3 * RWKV_C, 3 * RWKV_C + RWKV_W_LR + RWKV_A_LR


def _bdot(a, b):
    return jnp.dot(a.astype(BF16), b.astype(BF16), preferred_element_type=F32)


def _bdot_nt(a, b):
    return lax.dot_general(a.astype(BF16), b.astype(BF16), (((1,), (1,)), ((), ())),
                           preferred_element_type=F32)


def _bdot_tn(a, b):
    return lax.dot_general(a.astype(BF16), b.astype(BF16), (((0,), (0,)), ((), ())),
                           preferred_element_type=F32)


def _split3(x):
    hi = x.astype(BF16)
    r1 = x - hi.astype(F32)
    mid = r1.astype(BF16)
    lo = (r1 - mid.astype(F32)).astype(BF16)
    return hi, mid, lo


def _exact_dot(m_bf16, x):
    hi, mid, lo = _split3(x)
    d = lambda p: jnp.dot(m_bf16, p, preferred_element_type=F32)
    return d(hi) + d(mid) + d(lo)


def _head_sum(x):
    r = lax.broadcasted_iota(jnp.int32, (MXU_TILE, MXU_TILE), 0)
    c = lax.broadcasted_iota(jnp.int32, (MXU_TILE, MXU_TILE), 1)
    ones_blk = ((r // RWKV_N) == (c // RWKV_N)).astype(BF16)
    hi = x.astype(BF16)
    lo = (x - hi.astype(F32)).astype(BF16)
    out = []
    for g in range(x.shape[1] // MXU_TILE):
        cols = slice(g * MXU_TILE, (g + 1) * MXU_TILE)
        out.append(jnp.dot(hi[:, cols], ones_blk, preferred_element_type=F32)
                   + jnp.dot(lo[:, cols], ones_blk, preferred_element_type=F32))
    return jnp.concatenate(out, axis=1)


def _tril_incl(n):
    r = lax.broadcasted_iota(jnp.int32, (n, n), 0)
    c = lax.broadcasted_iota(jnp.int32, (n, n), 1)
    return (c <= r)


def _rmsnorm(x, g):
    return x * lax.rsqrt(jnp.mean(x * x, axis=-1, keepdims=True) + NORM_EPS) * g


def _shift_rows(x, k, fill_rows):
    y = pltpu.roll(x, k, 0)
    row = lax.broadcasted_iota(jnp.int32, x.shape, 0)
    for i, f in enumerate(fill_rows):
        y = jnp.where(row == i, f, y)
    return y


def _inproj_kernel(x_ref, g_ref, wg_ref, wr_ref, og_ref, or_ref):
    h = _rmsnorm(x_ref[...], g_ref[...]).astype(BF16)
    og_ref[...] = jnp.dot(h, wg_ref[...], preferred_element_type=F32)
    or_ref[...] = jnp.dot(h, wr_ref[...], preferred_element_type=F32)


def _inproj(x2d, g, w_gla, w_rwkv, tm):
    t = x2d.shape[0]
    return pl.pallas_call(
        _inproj_kernel,
        grid=(t // tm,),
        in_specs=[
            pl.BlockSpec((tm, D_MODEL), lambda i: (i, 0)),
            pl.BlockSpec((1, D_MODEL), lambda i: (0, 0)),
            pl.BlockSpec((D_MODEL, GLA_SLAB), lambda i: (0, 0)),
            pl.BlockSpec((D_MODEL, RWKV_COLS), lambda i: (0, 0)),
        ],
        out_specs=[
            pl.BlockSpec((tm, GLA_SLAB), lambda i: (i, 0)),
            pl.BlockSpec((tm, RWKV_COLS), lambda i: (i, 0)),
        ],
        out_shape=[
            jax.ShapeDtypeStruct((t, GLA_SLAB), F32),
            jax.ShapeDtypeStruct((t, RWKV_COLS), F32),
        ],
        compiler_params=pltpu.CompilerParams(
            dimension_semantics=("arbitrary",), vmem_limit_bytes=VMEM_LIMIT_BYTES),
        name="inproj",
    )(x2d, g, w_gla, w_rwkv)


def _gla_kernel(p_ref, w2_ref, b_ref, nw_ref, s0_ref, o_ref, sout_ref, st_ref, *, n_chunks):
    ti = pl.program_id(1)

    @pl.when(ti == 0)
    def _():
        st_ref[...] = s0_ref[0]

    tl = n_chunks * CHUNK
    ri = lax.broadcasted_iota(jnp.int32, (tl, tl), 0)
    ci = lax.broadcasted_iota(jnp.int32, (tl, tl), 1)
    same_chunk = (ri // CHUNK) == (ci // CHUNK)
    causal = same_chunk & (ci <= ri)
    lane = lax.broadcasted_iota(jnp.int32, (tl, LANES), 1)
    half = [lane < GLA_DK, lane >= GLA_DK]
    nw = nw_ref[...]

    logit = _bdot(p_ref[0, :, _GLR:_GLR + LANES], w2_ref[...]) + b_ref[...]
    la = jax.nn.log_sigmoid(logit) / GLA_GATE_NORMALIZER
    cum = _exact_dot(causal.astype(BF16), la)
    tot = _exact_dot(same_chunk.astype(BF16), la)
    q = p_ref[0, :, _GQ:_GQ + GLA_K] * (GLA_DK ** -0.5)
    k = p_ref[0, :, _GK:_GK + GLA_K]
    q_e = q * jnp.exp(cum)
    k_e = k * jnp.exp(-cum)
    k_hat = k * jnp.exp(tot - cum)
    dec = jnp.exp(tot)
    heads = range(GLA_HEADS)
    chunks = range(n_chunks)
    crow = lambda x, c: x[c * CHUNK:(c + 1) * CHUNK]
    pair = lambda x, h: x[:, (h // 2) * LANES:(h // 2 + 1) * LANES]
    v = [p_ref[0, :, _GV + h * GLA_DV:_GV + (h + 1) * GLA_DV] for h in heads]

    sc = [_bdot_nt(jnp.where(half[h % 2], pair(q_e, h), 0.0), pair(k_e, h)) for h in heads]
    o_intra = [_bdot(jnp.where(causal, sc[h], 0.0), v[h]) for h in heads]

    kh_m = [jnp.where(half[h % 2], pair(k_hat, h), 0.0) for h in heads]
    upd = [[_bdot_tn(crow(v[h], c), crow(kh_m[h], c)) for c in chunks] for h in heads]
    st_in = []
    for h in heads:
        st = st_ref[h]
        per_chunk = []
        for c in chunks:
            per_chunk.append(st)
            st = st * pair(dec, h)[c * CHUNK:c * CHUNK + 1] + upd[h][c]
        st_ref[h] = st
        st_in.append(per_chunk)
    o_inter = [jnp.concatenate([_bdot_nt(crow(pair(q_e, h), c), st_in[h][c]) for c in chunks], axis=0)
               for h in heads]

    for h in heads:
        o = o_intra[h] + o_inter[h]
        gg = p_ref[0, :, _GG + h * GLA_DV:_GG + (h + 1) * GLA_DV]
        o = o * lax.rsqrt(jnp.mean(o * o, axis=-1, keepdims=True) + NORM_EPS) * nw
        o = o * (gg * jax.nn.sigmoid(gg))
        o_ref[0, :, h * GLA_DV:(h + 1) * GLA_DV] = o.astype(o_ref.dtype)
    sout_ref[0] = st_ref[...]


def _gla(p_gla, w2p, gate_b, norm_w, st0, tl):
    b, l, _ = p_gla.shape
    return pl.pallas_call(
        functools.partial(_gla_kernel, n_chunks=tl // CHUNK),
        grid=(b, l // tl),
        in_specs=[
            pl.BlockSpec((1, tl, GLA_SLAB), lambda i, j: (i, j, 0)),
            pl.BlockSpec((LANES, GLA_K), lambda i, j: (0, 0)),
            pl.BlockSpec((1, GLA_K), lambda i, j: (0, 0)),
            pl.BlockSpec((1, GLA_DV), lambda i, j: (0, 0)),
            pl.BlockSpec((1, GLA_HEADS, GLA_DV, LANES), lambda i, j: (i, 0, 0, 0)),
        ],
        out_specs=[
            pl.BlockSpec((1, tl, GLA_V), lambda i, j: (i, j, 0)),
            pl.BlockSpec((1, GLA_HEADS, GLA_DV, LANES), lambda i, j: (i, 0, 0, 0)),
        ],
        out_shape=[
            jax.ShapeDtypeStruct((b, l, GLA_V), BF16),
            jax.ShapeDtypeStruct((b, GLA_HEADS, GLA_DV, LANES), F32),
        ],
        scratch_shapes=[pltpu.VMEM((GLA_HEADS, GLA_DV, LANES), F32)],
        compiler_params=pltpu.CompilerParams(
            dimension_semantics=("arbitrary", "arbitrary"), vmem_limit_bytes=VMEM_LIMIT_BYTES),
        name="gla",
    )(p_gla, w2p, gate_b, norm_w, st0)


def _rwkv_kernel(p_ref, sh_ref, mu_ref, w0_ref, w2_ref, a0_ref, a2_ref, g2_ref, kk_ref, ka_ref,
                 rk_ref, lnw_ref, lnb_ref, s0_ref,
                 y_ref, sout_ref, shout_ref,
                 s_ref, prev_ref, r_s, w_s, k_s, v_s, a_s, b_s, y_s, *, n_chunks, group, tl):
    ti = pl.program_id(1)

    @pl.when(ti == 0)
    def _():
        s_ref[...] = s0_ref[0]
        prev_ref[...] = sh_ref[0]

    seg_sum = _head_sum

    p = p_ref[0]
    prev = _shift_rows(p, 1, [prev_ref[...]])
    prev_ref[...] = p[tl - 1:tl, :]
    shout_ref[0] = p[tl - 1:tl, :]
    xs = p + mu_ref[...] * (prev - p)
    xr = xs[:, _RR:_RR + RWKV_C]
    xk = xs[:, _RK:_RK + RWKV_C]
    xv = xs[:, _RV:_RV + RWKV_C]
    xwa = xs[:, _RWA:_RWA + LANES]
    xg = xs[:, _RG:_RG + RWKV_G_LR]
    w_raw = w0_ref[...] + _bdot(jnp.tanh(xwa), w2_ref[...])
    w_log = -(math.exp(-0.5)) * jax.nn.sigmoid(w_raw)
    a = jax.nn.sigmoid(a0_ref[...] + _bdot(xwa, a2_ref[...]))
    g = _bdot(jax.nn.sigmoid(xg), g2_ref[...])
    kk = xk * kk_ref[...]
    kk = kk * lax.rsqrt(jnp.maximum(seg_sum(kk * kk), 1e-24))
    kr = xk * (1.0 + (a - 1.0) * ka_ref[...])
    bonus = seg_sum(xr * kr * rk_ref[...]) * xv
    r_s[...] = xr
    w_s[...] = w_log
    k_s[...] = kr
    v_s[...] = xv
    a_s[...] = -kk
    b_s[...] = kk * a

    tril_bf = _tril_incl(CHUNK).astype(BF16)
    lane = lax.broadcasted_iota(jnp.int32, (CHUNK, LANES), 1)
    lo_half = lane < RWKV_N
    ri = lax.broadcasted_iota(jnp.int32, (LANES, LANES), 0)
    ci = lax.broadcasted_iota(jnp.int32, (LANES, LANES), 1)
    strict = (ci & (RWKV_N - 1)) < (ri & (RWKV_N - 1))
    incl = (ci & (RWKV_N - 1)) <= (ri & (RWKV_N - 1))
    same_head = (ri >> 6) == (ci >> 6)
    eye = (ri == ci).astype(F32)

    def stack(x):
        return jnp.concatenate([jnp.where(lo_half, x, 0.0), jnp.where(lo_half, 0.0, x)], axis=0)

    def tile(x):
        return jnp.concatenate([x, x], axis=0)

    def fold(x):
        return jnp.where(lo_half, x[:CHUNK], x[CHUNK:])

    def body(gi, carry):
        units = []
        for cc in range(group):
            rows = pl.ds(pl.multiple_of((gi * group + cc) * CHUNK, CHUNK), CHUNK)
            lw_all = w_s[rows, :]
            cl_all = _exact_dot(tril_bf, lw_all)
            for pr in range(RWKV_PAIRS):
                units.append((rows, slice(pr * LANES, (pr + 1) * LANES), pr, lw_all, cl_all))
        n = range(len(units))
        a_t, r_t, b_t, k_t, b_h, k_h, v, dec = [], [], [], [], [], [], [], []
        for rows, cols, _, lw_all, cl_all in units:
            lw, cl = lw_all[:, cols], cl_all[:, cols]
            cl_end = cl[CHUNK - 1:CHUNK, :]
            e_neg = jnp.exp(-cl)
            e_end = jnp.exp(cl_end - cl)
            av, bv, kv = a_s[rows, cols], b_s[rows, cols], k_s[rows, cols]
            a_t.append(av * jnp.exp(cl - lw))
            r_t.append(r_s[rows, cols] * jnp.exp(cl))
            b_t.append(bv * e_neg)
            k_t.append(kv * e_neg)
            b_h.append(bv * e_end)
            k_h.append(kv * e_end)
            v.append(v_s[rows, cols])
            dec.append(jnp.exp(cl_end))
        gm = [_bdot_nt(jnp.concatenate([stack(a_t[i]), stack(r_t[i])], axis=0),
                       jnp.concatenate([stack(b_t[i]), stack(k_t[i])], axis=0)) for i in n]
        a_ab = [jnp.where(strict, gm[i][:LANES, :LANES], 0.0) for i in n]
        a_ak = [jnp.where(strict, gm[i][:LANES, LANES:], 0.0) for i in n]
        r_b = [jnp.where(incl, gm[i][LANES:, :LANES], 0.0) for i in n]
        r_k = [jnp.where(incl, gm[i][LANES:, LANES:], 0.0) for i in n]
        inv = [eye + a_ab[i] for i in n]
        pw = a_ab
        for _ in range(5):
            pw = [_bdot(pw[i], pw[i]) for i in n]
            inv = [inv[i] + _bdot(pw[i], inv[i]) for i in n]
        akv = [fold(_bdot(a_ak[i], tile(v[i]))) for i in n]
        w = [_bdot(inv[i], jnp.concatenate([stack(a_t[i]), tile(akv[i])], axis=1)) for i in n]
        a_bar = [fold(w[i][:, :LANES]) for i in n]
        u0 = [fold(w[i][:, LANES:]) for i in n]
        x = [_bdot(r_b[i], jnp.concatenate([stack(a_bar[i]), tile(u0[i])], axis=1)) for i in n]
        rkv = [_bdot(r_k[i], tile(v[i])) for i in n]
        r_bar = [r_t[i] + fold(x[i][:, :LANES]) for i in n]
        y0 = [fold(x[i][:, LANES:]) + fold(rkv[i]) for i in n]
        m1 = [jnp.where(same_head, _bdot_tn(a_bar[i], b_h[i]), 0.0) for i in n]
        z = [jnp.where(same_head, _bdot_tn(jnp.concatenate([u0[i], v[i]], axis=0),
                                           jnp.concatenate([b_h[i], k_h[i]], axis=0)), 0.0) for i in n]
        s = [s_ref[pr] for pr in range(RWKV_PAIRS)]
        for i in n:
            rows, cols, pr = units[i][:3]
            y_s[rows, cols] = _bdot_nt(r_bar[i], s[pr]) + y0[i]
            s[pr] = s[pr] * dec[i] + _bdot(s[pr], m1[i]) + z[i]
        for pr in range(RWKV_PAIRS):
            s_ref[pr] = s[pr]
        return carry

    lax.fori_loop(0, n_chunks // group, body, 0)
    sout_ref[0] = s_ref[...]

    y = y_s[...]
    mean = seg_sum(y) * (1.0 / RWKV_N)
    yc = y - mean
    var = seg_sum(yc * yc) * (1.0 / RWKV_N)
    y = yc * lax.rsqrt(var + RWKV_GN_EPS) * lnw_ref[...] + lnb_ref[...]
    y_ref[0] = ((y + bonus) * g).astype(y_ref.dtype)


def _rwkv(p_rwkv, shift_prev, prm, s0, tl):
    b, l, _ = p_rwkv.shape
    row = lambda n: pl.BlockSpec((1, n), lambda i, j: (0, 0))
    mat = lambda m, n: pl.BlockSpec((m, n), lambda i, j: (0, 0))
    return pl.pallas_call(
        functools.partial(_rwkv_kernel, n_chunks=tl // CHUNK, group=min(4, tl // CHUNK), tl=tl),
        grid=(b, l // tl),
        in_specs=[
            pl.BlockSpec((1, tl, RWKV_COLS), lambda i, j: (i, j, 0)),
            pl.BlockSpec((1, 1, RWKV_COLS), lambda i, j: (i, 0, 0)),
            row(RWKV_COLS),
            row(RWKV_C), mat(LANES, RWKV_C), row(RWKV_C), mat(LANES, RWKV_C), mat(RWKV_G_LR, RWKV_C),
            row(RWKV_C), row(RWKV_C), row(RWKV_C), row(RWKV_C), row(RWKV_C),
            pl.BlockSpec((1, RWKV_PAIRS, LANES, LANES), lambda i, j: (i, 0, 0, 0)),
        ],
        out_specs=[
            pl.BlockSpec((1, tl, RWKV_C), lambda i, j: (i, j, 0)),
            pl.BlockSpec((1, RWKV_PAIRS, LANES, LANES), lambda i, j: (i, 0, 0, 0)),
            pl.BlockSpec((1, 1, RWKV_COLS), lambda i, j: (i, 0, 0)),
        ],
        out_shape=[
            jax.ShapeDtypeStruct((b, l, RWKV_C), BF16),
            jax.ShapeDtypeStruct((b, RWKV_PAIRS, LANES, LANES), F32),
            jax.ShapeDtypeStruct((b, 1, RWKV_COLS), F32),
        ],
        scratch_shapes=[
            pltpu.VMEM((RWKV_PAIRS, LANES, LANES), F32),
            pltpu.VMEM((1, RWKV_COLS), F32),
        ] + [pltpu.VMEM((tl, RWKV_C), F32)] * 7,
        compiler_params=pltpu.CompilerParams(
            dimension_semantics=("arbitrary", "arbitrary"), vmem_limit_bytes=VMEM_LIMIT_BYTES),
        name="rwkv",
    )(p_rwkv, shift_prev, prm["mu"], prm["w0"], prm["w2p"], prm["a0"], prm["a2p"], prm["g2"],
      prm["k_k"], prm["k_a"], prm["r_k"], prm["ln_w"], prm["ln_b"], s0)


def _post_kernel(x_ref, og_ref, yr_ref, wo_ref, nf_ref, wu_ref, wg_ref, cw_ref, cb_ref, wd_ref,
                 cprev_ref, nfin_ref, out_ref, cnew_ref, act_ref, carry_ref,
                 *, tiles_per_seq, tm, final_norm):
    i = pl.program_id(0)
    wo = wo_ref[...]
    mix = (jnp.dot(og_ref[...], wo[:GLA_V], preferred_element_type=F32)
           + jnp.dot(yr_ref[...], wo[GLA_V:], preferred_element_type=F32))
    x1 = x_ref[...] + mix
    out_ref[...] = x1
    h = _rmsnorm(x1, nf_ref[...]).astype(BF16)

    @pl.when(i % tiles_per_seq == 0)
    def _():
        carry_ref[...] = cprev_ref[0]

    for j in range(D_FF // MXU_TILE):
        cols = slice(j * MXU_TILE, (j + 1) * MXU_TILE)
        u = jnp.dot(h, wu_ref[:, cols], preferred_element_type=F32)
        gate = jnp.dot(h, wg_ref[:, cols], preferred_element_type=F32)
        p0, p1 = carry_ref[0:1, cols], carry_ref[1:2, cols]
        uc = (cb_ref[:, cols] + cw_ref[0:1, cols] * _shift_rows(u, 2, [p0, p1])
              + cw_ref[1:2, cols] * _shift_rows(u, 1, [p1]) + cw_ref[2:3, cols] * u)
        new_prev = u[tm - 2:tm, :]
        carry_ref[:, cols] = new_prev
        cnew_ref[0, :, cols] = new_prev
        act = 0.5 * uc * (1.0 + lax.erf(uc * (2.0 ** -0.5))) * gate
        act_ref[:, cols] = act.astype(BF16)

    x2 = out_ref[...] + jnp.dot(act_ref[...], wd_ref[...], preferred_element_type=F32)
    if final_norm:
        x2 = _rmsnorm(x2, nfin_ref[...])
    out_ref[...] = x2


def _post(x2d, o_gla, y_rwkv, prm, conv_prev, tm, seq_len, final_norm):
    t = x2d.shape[0]
    b = conv_prev.shape[0]
    tiles_per_seq = seq_len // tm
    const = lambda m, n: pl.BlockSpec((m, n), lambda i: (0, 0), pipeline_mode=pl.Buffered(1))
    return pl.pallas_call(
        functools.partial(_post_kernel, tiles_per_seq=tiles_per_seq, tm=tm, final_norm=final_norm),
        grid=(t // tm,),
        in_specs=[
            pl.BlockSpec((tm, D_MODEL), lambda i: (i, 0)),
            pl.BlockSpec((tm, GLA_V), lambda i: (i, 0)),
            pl.BlockSpec((tm, RWKV_C), lambda i: (i, 0)),
            const(D_MODEL, D_MODEL),
            const(1, D_MODEL),
            const(D_MODEL, D_FF),
            const(D_MODEL, D_FF),
            const(CONV_W, D_FF),
            const(1, D_FF),
            const(D_FF, D_MODEL),
            pl.BlockSpec((1, CONV_W - 1, D_FF), lambda i: (i // tiles_per_seq, 0, 0)),
            const(1, D_MODEL),
        ],
        out_specs=[
            pl.BlockSpec((tm, D_MODEL), lambda i: (i, 0)),
            pl.BlockSpec((1, CONV_W - 1, D_FF), lambda i: (i // tiles_per_seq, 0, 0)),
        ],
        out_shape=[
            jax.ShapeDtypeStruct((t, D_MODEL), F32),
            jax.ShapeDtypeStruct((b, CONV_W - 1, D_FF), F32),
        ],
        scratch_shapes=[
            pltpu.VMEM((tm, D_FF), BF16),
            pltpu.VMEM((CONV_W - 1, D_FF), F32),
        ],
        compiler_params=pltpu.CompilerParams(
            dimension_semantics=("arbitrary",), vmem_limit_bytes=VMEM_LIMIT_BYTES),
        name="post",
    )(x2d, o_gla, y_rwkv, prm["w_out"], prm["norm_ffn"], prm["w_up_u"], prm["w_up_g"],
      prm["conv_w"], prm["conv_b"], prm["w_down"], conv_prev, prm["norm_final"])


def _layer_params(l, P):
    w_in = P["w_in"][l]
    zpad = lambda a, rows_before, rows_total: jnp.zeros((rows_total, a.shape[1]), a.dtype).at[
        rows_before:rows_before + a.shape[0]].set(a)
    w_gla = jnp.concatenate(
        [w_in[:, :GLA_COLS], jnp.zeros((D_MODEL, GLA_SLAB - GLA_COLS), w_in.dtype)], axis=1)
    r = lambda a: a.reshape(1, -1)
    return dict(
        norm_mix=r(P["norm_mix"][l]),
        w_gla=w_gla.astype(BF16),
        w_rwkv=w_in[:, GLA_COLS:].astype(BF16),
        gate_w2p=zpad(P["gla_gate_w2"][l], 0, LANES).astype(BF16),
        gate_b=r(P["gla_gate_b"][l]),
        gla_norm_w=r(P["gla_norm_w"][l]),
        mu=r(P["rwkv_mu"][l]),
        w0=r(P["rwkv_w0"][l]),
        w2p=zpad(P["rwkv_w2"][l], 0, LANES).astype(BF16),
        a0=r(P["rwkv_a0"][l]),
        a2p=zpad(P["rwkv_a2"][l], RWKV_W_LR, LANES).astype(BF16),
        g2=P["rwkv_g2"][l].astype(BF16),
        k_k=r(P["rwkv_k_k"][l]), k_a=r(P["rwkv_k_a"][l]), r_k=r(P["rwkv_r_k"][l]),
        ln_w=r(P["rwkv_ln_w"][l]), ln_b=r(P["rwkv_ln_b"][l]),
        w_out=P["w_out"][l].astype(BF16),
        norm_ffn=r(P["norm_ffn"][l]),
        w_up_u=P["ffn_w_up"][l][:, :D_FF].astype(BF16),
        w_up_g=P["ffn_w_up"][l][:, D_FF:].astype(BF16),
        conv_w=P["ffn_conv_w"][l], conv_b=r(P["ffn_conv_b"][l]),
        w_down=P["ffn_w_down"][l].astype(BF16),
        norm_final=r(P["norm_final"]),
    )


def _gla_state_in(s):
    st = jnp.swapaxes(s, -1, -2)
    z = jnp.zeros_like(st)
    par = (jnp.arange(GLA_HEADS) % 2).reshape(1, GLA_HEADS, 1, 1)
    return jnp.where(par == 0, jnp.concatenate([st, z], -1), jnp.concatenate([z, st], -1))


def _gla_state_out(st):
    lo, hi = st[..., :GLA_DK], st[..., GLA_DK:]
    par = (jnp.arange(GLA_HEADS) % 2).reshape(1, GLA_HEADS, 1, 1)
    return jnp.swapaxes(jnp.where(par == 0, lo, hi), -1, -2)


def _rwkv_state_in(s):
    b = s.shape[0]
    s = s.reshape(b, RWKV_PAIRS, 2, RWKV_N, RWKV_N)
    z = jnp.zeros_like(s[:, :, 0])
    top = jnp.concatenate([s[:, :, 0], z], -1)
    bot = jnp.concatenate([z, s[:, :, 1]], -1)
    return jnp.concatenate([top, bot], -2)


def _rwkv_state_out(sb):
    b = sb.shape[0]
    s0 = sb[:, :, :RWKV_N, :RWKV_N]
    s1 = sb[:, :, RWKV_N:, RWKV_N:]
    return jnp.stack([s0, s1], axis=2).reshape(b, RWKV_HEADS, RWKV_N, RWKV_N)


def _tiles(seq_len):
    tm = min(512, seq_len)
    tl = min(256, seq_len)
    return tm, tl


def _trunk(x, s_gla, s_rwkv, s_shift, c_conv, layers):
    b, l, _ = x.shape
    tm, tl = _tiles(l)
    x2d = x.reshape(b * l, D_MODEL)
    new_gla, new_rwkv, new_shift, new_conv = [], [], [], []
    for li, prm in enumerate(layers):
        p_gla, p_rwkv = _inproj(x2d, prm["norm_mix"], prm["w_gla"], prm["w_rwkv"], tm)
        o_gla, st = _gla(p_gla.reshape(b, l, GLA_SLAB), prm["gate_w2p"], prm["gate_b"],
                         prm["gla_norm_w"], _gla_state_in(s_gla[li]), tl)
        y_rwkv, sb, sh = _rwkv(p_rwkv.reshape(b, l, RWKV_COLS), s_shift[li][:, None, :], prm,
                               _rwkv_state_in(s_rwkv[li]), tl)
        x2d, cc = _post(x2d, o_gla.reshape(b * l, GLA_V), y_rwkv.reshape(b * l, RWKV_C), prm,
                        c_conv[li], tm, l, li == len(layers) - 1)
        new_gla.append(_gla_state_out(st))
        new_rwkv.append(_rwkv_state_out(sb))
        new_shift.append(sh[:, 0, :])
        new_conv.append(cc)
    return (x2d.reshape(b, l, D_MODEL), jnp.stack(new_gla), jnp.stack(new_rwkv),
            jnp.stack(new_shift), jnp.stack(new_conv))


def kernel(x_prompt, x_sample, state_gla, state_rwkv, state_rwkv_shift, cache_ffn_conv, norm_mix, w_in, gla_gate_w2, gla_gate_b, gla_norm_w, rwkv_mu, rwkv_w0, rwkv_w2, rwkv_a0, rwkv_a2, rwkv_g2, rwkv_k_k, rwkv_k_a, rwkv_r_k, rwkv_ln_w, rwkv_ln_b, w_out, norm_ffn, ffn_w_up, ffn_conv_w, ffn_conv_b, ffn_w_down, norm_final):
    P = dict(norm_mix=norm_mix, w_in=w_in, gla_gate_w2=gla_gate_w2, gla_gate_b=gla_gate_b,
             gla_norm_w=gla_norm_w, rwkv_mu=rwkv_mu, rwkv_w0=rwkv_w0, rwkv_w2=rwkv_w2,
             rwkv_a0=rwkv_a0, rwkv_a2=rwkv_a2, rwkv_g2=rwkv_g2, rwkv_k_k=rwkv_k_k,
             rwkv_k_a=rwkv_k_a, rwkv_r_k=rwkv_r_k.reshape(DEPTH, RWKV_C), rwkv_ln_w=rwkv_ln_w,
             rwkv_ln_b=rwkv_ln_b, w_out=w_out, norm_ffn=norm_ffn, ffn_w_up=ffn_w_up,
             ffn_conv_w=ffn_conv_w, ffn_conv_b=ffn_conv_b, ffn_w_down=ffn_w_down,
             norm_final=norm_final)
    layers = [_layer_params(l, P) for l in range(DEPTH)]
    nb = x_prompt.shape[0]
    g0 = jnp.zeros((DEPTH, nb) + state_gla.shape[2:], state_gla.dtype)
    r0 = jnp.zeros((DEPTH, nb) + state_rwkv.shape[2:], state_rwkv.dtype)
    sh0 = jnp.zeros((DEPTH, nb) + state_rwkv_shift.shape[2:], state_rwkv_shift.dtype)
    c0 = jnp.zeros((DEPTH, nb) + cache_ffn_conv.shape[2:], cache_ffn_conv.dtype)
    y_p, gla_p, rwkv_p, shift_p, conv_p = _trunk(x_prompt, g0, r0, sh0, c0, layers)
    y_s, gla_s, rwkv_s, shift_s, conv_s = _trunk(
        x_sample, state_gla, state_rwkv, state_rwkv_shift, cache_ffn_conv, layers)
    return (y_p, y_s, gla_p, rwkv_p, shift_p, conv_p, gla_s, rwkv_s, shift_s, conv_s)
```

```python
import functools
import math

import jax
import jax.numpy as jnp
from jax import lax
from jax.experimental import pallas as pl
from jax.experimental.pallas import tpu as pltpu

F32 = jnp.float32
BF16 = jnp.bfloat16

D_MODEL = 1024
DEPTH = 2
CHUNK = 64
NORM_EPS = 1e-6
GLA_HEADS = 4
GLA_DV = 128
GLA_DK = 64
GLA_K = GLA_HEADS * GLA_DK
GLA_V = GLA_HEADS * GLA_DV
GLA_LR = 16
GLA_GATE_NORMALIZER = 16.0
RWKV_N = 64
RWKV_C = 512
RWKV_HEADS = 8
RWKV_PAIRS = RWKV_HEADS // 2
RWKV_W_LR = 64
RWKV_A_LR = 64
RWKV_G_LR = 128
RWKV_GN_EPS = 64e-5
GLA_COLS = 2 * GLA_K + 2 * GLA_V + GLA_LR
RWKV_COLS = 3 * RWKV_C + RWKV_W_LR + RWKV_A_LR + RWKV_G_LR
D_FF = 2816
CONV_W = 3

LANES = 128
MXU_TILE = 256
GLA_SLAB = 2 * GLA_K + 2 * GLA_V + LANES
VMEM_LIMIT_BYTES = 56 * 1024 * 1024

_GQ, _GK, _GV, _GG, _GLR = 0, GLA_K, 2 * GLA_K, 2 * GLA_K + GLA_V, 2 * GLA_K + 2 * GLA_V
_RR, _RK, _RV, _RWA, _RG = 0, RWKV_C, 2 * RWKV_C, 3 * RWKV_C, 3 * RWKV_C + RWKV_W_LR + RWKV_A_LR


def _bdot(a, b):
    return jnp.dot(a.astype(BF16), b.astype(BF16), preferred_element_type=F32)


def _bdot_nt(a, b):
    return lax.dot_general(a.astype(BF16), b.astype(BF16), (((1,), (1,)), ((), ())),
                           preferred_element_type=F32)


def _bdot_tn(a, b):
    return lax.dot_general(a.astype(BF16), b.astype(BF16), (((0,), (0,)), ((), ())),
                           preferred_element_type=F32)


def _split3(x):
    hi = x.astype(BF16)
    r1 = x - hi.astype(F32)
    mid = r1.astype(BF16)
    lo = (r1 - mid.astype(F32)).astype(BF16)
    return hi, mid, lo


def _exact_dot(m_bf16, x):
    hi, mid, lo = _split3(x)
    d = lambda p: jnp.dot(m_bf16, p, preferred_element_type=F32)
    return d(hi) + d(mid) + d(lo)


def _head_sum(x):
    r = lax.broadcasted_iota(jnp.int32, (MXU_TILE, MXU_TILE), 0)
    c = lax.broadcasted_iota(jnp.int32, (MXU_TILE, MXU_TILE), 1)
    ones_blk = ((r // RWKV_N) == (c // RWKV_N)).astype(BF16)
    xb = x.astype(BF16)
    out = [jnp.dot(xb[:, g * MXU_TILE:(g + 1) * MXU_TILE], ones_blk, preferred_element_type=F32)
           for g in range(x.shape[1] // MXU_TILE)]
    return jnp.concatenate(out, axis=1)


def _tril_incl(n):
    r = lax.broadcasted_iota(jnp.int32, (n, n), 0)
    c = lax.broadcasted_iota(jnp.int32, (n, n), 1)
    return (c <= r)


def _rmsnorm(x, g):
    return x * lax.rsqrt(jnp.mean(x * x, axis=-1, keepdims=True) + NORM_EPS) * g


def _shift_rows(x, k, fill_rows):
    y = pltpu.roll(x, k, 0)
    row = lax.broadcasted_iota(jnp.int32, x.shape, 0)
    for i, f in enumerate(fill_rows):
        y = jnp.where(row == i, f, y)
    return y


def _inproj_kernel(x_ref, g_ref, wg_ref, wr_ref, og_ref, or_ref):
    h = _rmsnorm(x_ref[...], g_ref[...]).astype(BF16)
    og_ref[...] = jnp.dot(h, wg_ref[...], preferred_element_type=F32)
    or_ref[...] = jnp.dot(h, wr_ref[...], preferred_element_type=F32)


def _inproj(x2d, g, w_gla, w_rwkv, tm):
    t = x2d.shape[0]
    return pl.pallas_call(
        _inproj_kernel,
        grid=(t // tm,),
        in_specs=[
            pl.BlockSpec((tm, D_MODEL), lambda i: (i, 0)),
            pl.BlockSpec((1, D_MODEL), lambda i: (0, 0)),
            pl.BlockSpec((D_MODEL, GLA_SLAB), lambda i: (0, 0)),
            pl.BlockSpec((D_MODEL, RWKV_COLS), lambda i: (0, 0)),
        ],
        out_specs=[
            pl.BlockSpec((tm, GLA_SLAB), lambda i: (i, 0)),
            pl.BlockSpec((tm, RWKV_COLS), lambda i: (i, 0)),
        ],
        out_shape=[
            jax.ShapeDtypeStruct((t, GLA_SLAB), F32),
            jax.ShapeDtypeStruct((t, RWKV_COLS), F32),
        ],
        compiler_params=pltpu.CompilerParams(
            dimension_semantics=("arbitrary",), vmem_limit_bytes=VMEM_LIMIT_BYTES),
        name="inproj",
    )(x2d, g, w_gla, w_rwkv)


def _gla_kernel(p_ref, w2_ref, b_ref, nw_ref, s0_ref, o_ref, sout_ref, st_ref, *, n_chunks):
    ti = pl.program_id(1)

    @pl.when(ti == 0)
    def _():
        st_ref[...] = s0_ref[0]

    tl = n_chunks * CHUNK
    ri = lax.broadcasted_iota(jnp.int32, (tl, tl), 0)
    ci = lax.broadcasted_iota(jnp.int32, (tl, tl), 1)
    same_chunk = (ri // CHUNK) == (ci // CHUNK)
    causal = same_chunk & (ci <= ri)
    lane = lax.broadcasted_iota(jnp.int32, (tl, LANES), 1)
    half = [lane < GLA_DK, lane >= GLA_DK]
    nw = nw_ref[...]

    logit = _bdot(p_ref[0, :, _GLR:_GLR + LANES], w2_ref[...]) + b_ref[...]
    la = jax.nn.log_sigmoid(logit) / GLA_GATE_NORMALIZER
    cum = _exact_dot(causal.astype(BF16), la)
    tot = _exact_dot(same_chunk.astype(BF16), la)
    q = p_ref[0, :, _GQ:_GQ + GLA_K] * (GLA_DK ** -0.5)
    k = p_ref[0, :, _GK:_GK + GLA_K]
    q_e = q * jnp.exp(cum)
    k_e = k * jnp.exp(-cum)
    k_hat = k * jnp.exp(tot - cum)
    dec = jnp.exp(tot)
    heads = range(GLA_HEADS)
    chunks = range(n_chunks)
    crow = lambda x, c: x[c * CHUNK:(c + 1) * CHUNK]
    pair = lambda x, h: x[:, (h // 2) * LANES:(h // 2 + 1) * LANES]
    v = [p_ref[0, :, _GV + h * GLA_DV:_GV + (h + 1) * GLA_DV] for h in heads]

    sc = [_bdot_nt(jnp.where(half[h % 2], pair(q_e, h), 0.0), pair(k_e, h)) for h in heads]
    o_intra = [_bdot(jnp.where(causal, sc[h], 0.0), v[h]) for h in heads]

    kh_m = [jnp.where(half[h % 2], pair(k_hat, h), 0.0) for h in heads]
    upd = [[_bdot_tn(crow(v[h], c), crow(kh_m[h], c)) for c in chunks] for h in heads]
    st_in = []
    for h in heads:
        st = st_ref[h]
        per_chunk = []
        for c in chunks:
            per_chunk.append(st)
            st = st * pair(dec, h)[c * CHUNK:c * CHUNK + 1] + upd[h][c]
        st_ref[h] = st
        st_in.append(per_chunk)
    o_inter = [jnp.concatenate([_bdot_nt(crow(pair(q_e, h), c), st_in[h][c]) for c in chunks], axis=0)
               for h in heads]

    for h in heads:
        o = o_intra[h] + o_inter[h]
        gg = p_ref[0, :, _GG + h * GLA_DV:_GG + (h + 1) * GLA_DV]
        o = o * lax.rsqrt(jnp.mean(o * o, axis=-1, keepdims=True) + NORM_EPS) * nw
        o = o * (gg * jax.nn.sigmoid(gg))
        o_ref[0, :, h * GLA_DV:(h + 1) * GLA_DV] = o.astype(o_ref.dtype)
    sout_ref[0] = st_ref[...]


def _gla(p_gla, w2p, gate_b, norm_w, st0, tl):
    b, l, _ = p_gla.shape
    return pl.pallas_call(
        functools.partial(_gla_kernel, n_chunks=tl // CHUNK),
        grid=(b, l // tl),
        in_specs=[
            pl.BlockSpec((1, tl, GLA_SLAB), lambda i, j: (i, j, 0)),
            pl.BlockSpec((LANES, GLA_K), lambda i, j: (0, 0)),
            pl.BlockSpec((1, GLA_K), lambda i, j: (0, 0)),
            pl.BlockSpec((1, GLA_DV), lambda i, j: (0, 0)),
            pl.BlockSpec((1, GLA_HEADS, GLA_DV, LANES), lambda i, j: (i, 0, 0, 0)),
        ],
        out_specs=[
            pl.BlockSpec((1, tl, GLA_V), lambda i, j: (i, j, 0)),
            pl.BlockSpec((1, GLA_HEADS, GLA_DV, LANES), lambda i, j: (i, 0, 0, 0)),
        ],
        out_shape=[
            jax.ShapeDtypeStruct((b, l, GLA_V), BF16),
            jax.ShapeDtypeStruct((b, GLA_HEADS, GLA_DV, LANES), F32),
        ],
        scratch_shapes=[pltpu.VMEM((GLA_HEADS, GLA_DV, LANES), F32)],
        compiler_params=pltpu.CompilerParams(
            dimension_semantics=("arbitrary", "arbitrary"), vmem_limit_bytes=VMEM_LIMIT_BYTES),
        name="gla",
    )(p_gla, w2p, gate_b, norm_w, st0)


def _rwkv_kernel(p_ref, sh_ref, mu_ref, w0_ref, w2_ref, a0_ref, a2_ref, g2_ref, kk_ref, ka_ref,
                 rk_ref, lnw_ref, lnb_ref, s0_ref,
                 y_ref, sout_ref, shout_ref,
                 s_ref, prev_ref, *, n_chunks, tl):
    ti = pl.program_id(1)

    @pl.when(ti == 0)
    def _():
        s_ref[...] = s0_ref[0]
        prev_ref[...] = sh_ref[0]

    seg_sum = _head_sum

    p = p_ref[0]
    prev = _shift_rows(p, 1, [prev_ref[...]])
    prev_ref[...] = p[tl - 1:tl, :]
    shout_ref[0] = p[tl - 1:tl, :]
    xs = p + mu_ref[...] * (prev - p)
    xr = xs[:, _RR:_RR + RWKV_C]
    xk = xs[:, _RK:_RK + RWKV_C]
    xv = xs[:, _RV:_RV + RWKV_C]
    xwa = xs[:, _RWA:_RWA + LANES]
    xg = xs[:, _RG:_RG + RWKV_G_LR]
    w_raw = w0_ref[...] + _bdot(jnp.tanh(xwa), w2_ref[...])
    w_log = -(math.exp(-0.5)) * jax.nn.sigmoid(w_raw)
    a = jax.nn.sigmoid(a0_ref[...] + _bdot(xwa, a2_ref[...]))
    g = _bdot(jax.nn.sigmoid(xg), g2_ref[...])
    kk = xk * kk_ref[...]
    kk = kk * lax.rsqrt(jnp.maximum(seg_sum(kk * kk), 1e-24))
    kr = xk * (1.0 + (a - 1.0) * ka_ref[...])
    bonus = seg_sum(xr * kr * rk_ref[...]) * xv
    a_vec = -kk
    b_vec = kk * a
    tr = lax.broadcasted_iota(jnp.int32, (tl, tl), 0)
    tc = lax.broadcasted_iota(jnp.int32, (tl, tl), 1)
    cl_tile = _exact_dot((((tr // CHUNK) == (tc // CHUNK)) & (tc <= tr)).astype(BF16), w_log)

    trow = lax.broadcasted_iota(jnp.int32, (CHUNK, LANES), 0)
    tcol = lax.broadcasted_iota(jnp.int32, (CHUNK, LANES), 1) & (RWKV_N - 1)
    strict = tcol < trow
    incl = tcol <= trow
    eye = (tcol == trow).astype(F32)
    ri = lax.broadcasted_iota(jnp.int32, (LANES, LANES), 0)
    ci = lax.broadcasted_iota(jnp.int32, (LANES, LANES), 1)
    same_head = (ri >> 6) == (ci >> 6)
    zeros_bd = jnp.zeros((LANES, LANES), BF16)

    def blockdiag(x):
        xb = x.astype(BF16)
        return jnp.where(same_head, jnp.concatenate([xb, xb], axis=0), zeros_bd)

    units = [(slice(c * CHUNK, (c + 1) * CHUNK), slice(pr * LANES, (pr + 1) * LANES))
             for c in range(n_chunks) for pr in range(RWKV_PAIRS)]
    n = range(len(units))
    a_t, r_t, b_t, k_t, b_h, k_h, v, dec = [], [], [], [], [], [], [], []
    for rows, cols in units:
        lw, cl = w_log[rows, cols], cl_tile[rows, cols]
        dec_row = jnp.exp(cl[CHUNK - 1:CHUNK, :])
        e_neg = jnp.exp(-cl)
        e_end = dec_row * e_neg
        av, bv, kv = a_vec[rows, cols], b_vec[rows, cols], kr[rows, cols]
        a_t.append(av * jnp.exp(cl - lw))
        r_t.append(xr[rows, cols] * jnp.exp(cl))
        b_t.append(bv * e_neg)
        k_t.append(kv * e_neg)
        b_h.append(bv * e_end)
        k_h.append(kv * e_end)
        v.append(xv[rows, cols])
        dec.append(dec_row)
    gm = [_bdot_nt(jnp.concatenate([a_t[i], r_t[i]], axis=0),
                   jnp.concatenate([blockdiag(b_t[i]), blockdiag(k_t[i])], axis=0)) for i in n]
    a_ab = [jnp.where(strict, gm[i][:CHUNK, :LANES], 0.0) for i in n]
    a_ak = [jnp.where(strict, gm[i][:CHUNK, LANES:], 0.0) for i in n]
    r_b = [jnp.where(incl, gm[i][CHUNK:, :LANES], 0.0) for i in n]
    r_k = [jnp.where(incl, gm[i][CHUNK:, LANES:], 0.0) for i in n]
    inv = [eye + a_ab[i] for i in n]
    pw = a_ab
    for _ in range(5):
        pw = [_bdot(pw[i], blockdiag(pw[i])) for i in n]
        inv = [inv[i] + _bdot(pw[i], blockdiag(inv[i])) for i in n]
    v_bd = [blockdiag(v[i]) for i in n]
    akv = [_bdot(a_ak[i], v_bd[i]) for i in n]
    w = [_bdot(inv[i], jnp.concatenate([blockdiag(a_t[i]), blockdiag(akv[i])], axis=1))
         for i in n]
    a_bar = [w[i][:, :LANES] for i in n]
    u0 = [w[i][:, LANES:] for i in n]
    x = [_bdot(jnp.concatenate([r_b[i], r_k[i]], axis=1),
               jnp.concatenate(
                   [jnp.concatenate([blockdiag(a_bar[i]), blockdiag(u0[i])], axis=1),
                    jnp.concatenate([zeros_bd, v_bd[i]], axis=1)], axis=0)) for i in n]
    r_bar = [r_t[i] + x[i][:, :LANES] for i in n]
    y0 = [x[i][:, LANES:] for i in n]
    m1 = [jnp.where(same_head, _bdot_tn(a_bar[i], b_h[i]), 0.0) for i in n]
    z = [jnp.where(same_head, _bdot_tn(jnp.concatenate([u0[i], v[i]], axis=0),
                                       jnp.concatenate([b_h[i], k_h[i]], axis=0)), 0.0) for i in n]
    y_chunks = []
    for c in range(n_chunks):
        y_pairs = []
        for pr in range(RWKV_PAIRS):
            i = c * RWKV_PAIRS + pr
            s = s_ref[pr]
            y_pairs.append(_bdot_nt(r_bar[i], s) + y0[i])
            s_ref[pr] = s * dec[i] + _bdot(s, m1[i]) + z[i]
        y_chunks.append(jnp.concatenate(y_pairs, axis=1))
    sout_ref[0] = s_ref[...]

    y = jnp.concatenate(y_chunks, axis=0)
    mean = seg_sum(y) * (1.0 / RWKV_N)
    yc = y - mean
    var = seg_sum(yc * yc) * (1.0 / RWKV_N)
    y = yc * lax.rsqrt(var + RWKV_GN_EPS) * lnw_ref[...] + lnb_ref[...]
    y_ref[0] = ((y + bonus) * g).astype(y_ref.dtype)


def _rwkv(p_rwkv, shift_prev, prm, s0, tl):
    b, l, _ = p_rwkv.shape
    row = lambda n: pl.BlockSpec((1, n), lambda i, j: (0, 0))
    mat = lambda m, n: pl.BlockSpec((m, n), lambda i, j: (0, 0))
    return pl.pallas_call(
        functools.partial(_rwkv_kernel, n_chunks=tl // CHUNK, tl=tl),
        grid=(b, l // tl),
        in_specs=[
            pl.BlockSpec((1, tl, RWKV_COLS), lambda i, j: (i, j, 0)),
            pl.BlockSpec((1, 1, RWKV_COLS), lambda i, j: (i, 0, 0)),
            row(RWKV_COLS),
            row(RWKV_C), mat(LANES, RWKV_C), row(RWKV_C), mat(LANES, RWKV_C), mat(RWKV_G_LR, RWKV_C),
            row(RWKV_C), row(RWKV_C), row(RWKV_C), row(RWKV_C), row(RWKV_C),
            pl.BlockSpec((1, RWKV_PAIRS, LANES, LANES), lambda i, j: (i, 0, 0, 0)),
        ],
        out_specs=[
            pl.BlockSpec((1, tl, RWKV_C), lambda i, j: (i, j, 0)),
            pl.BlockSpec((1, RWKV_PAIRS, LANES, LANES), lambda i, j: (i, 0, 0, 0)),
            pl.BlockSpec((1, 1, RWKV_COLS), lambda i, j: (i, 0, 0)),
        ],
        out_shape=[
            jax.ShapeDtypeStruct((b, l, RWKV_C), BF16),
            jax.ShapeDtypeStruct((b, RWKV_PAIRS, LANES, LANES), F32),
            jax.ShapeDtypeStruct((b, 1, RWKV_COLS), F32),
        ],
        scratch_shapes=[
            pltpu.VMEM((RWKV_PAIRS, LANES, LANES), F32),
            pltpu.VMEM((1, RWKV_COLS), F32),
        ],
        compiler_params=pltpu.CompilerParams(
            dimension_semantics=("arbitrary", "arbitrary"), vmem_limit_bytes=VMEM_LIMIT_BYTES),
        name="rwkv",
    )(p_rwkv, shift_prev, prm["mu"], prm["w0"], prm["w2p"], prm["a0"], prm["a2p"], prm["g2"],
      prm["k_k"], prm["k_a"], prm["r_k"], prm["ln_w"], prm["ln_b"], s0)


def _post_kernel(x_ref, og_ref, yr_ref, wo_ref, nf_ref, wu_ref, wg_ref, cw_ref, cb_ref, wd_ref,
                 cprev_ref, nfin_ref, out_ref, cnew_ref, act_ref, carry_ref,
                 *, tiles_per_seq, tm, final_norm):
    i = pl.program_id(0)
    wo = wo_ref[...]
    mix = (jnp.dot(og_ref[...], wo[:GLA_V], preferred_element_type=F32)
           + jnp.dot(yr_ref[...], wo[GLA_V:], preferred_element_type=F32))
    x1 = x_ref[...] + mix
    out_ref[...] = x1
    h = _rmsnorm(x1, nf_ref[...]).astype(BF16)

    @pl.when(i % tiles_per_seq == 0)
    def _():
        carry_ref[...] = cprev_ref[0]

    for j in range(D_FF // MXU_TILE):
        cols = slice(j * MXU_TILE, (j + 1) * MXU_TILE)
        u = jnp.dot(h, wu_ref[:, cols], preferred_element_type=F32)
        gate = jnp.dot(h, wg_ref[:, cols], preferred_element_type=F32)
        p0, p1 = carry_ref[0:1, cols], carry_ref[1:2, cols]
        uc = (cb_ref[:, cols] + cw_ref[0:1, cols] * _shift_rows(u, 2, [p0, p1])
              + cw_ref[1:2, cols] * _shift_rows(u, 1, [p1]) + cw_ref[2:3, cols] * u)
        new_prev = u[tm - 2:tm, :]
        carry_ref[:, cols] = new_prev
        cnew_ref[0, :, cols] = new_prev
        act = 0.5 * uc * (1.0 + lax.erf(uc * (2.0 ** -0.5))) * gate
        act_ref[:, cols] = act.astype(BF16)

    x2 = out_ref[...] + jnp.dot(act_ref[...], wd_ref[...], preferred_element_type=F32)
    if final_norm:
        x2 = _rmsnorm(x2, nfin_ref[...])
    out_ref[...] = x2


def _post(x2d, o_gla, y_rwkv, prm, conv_prev, tm, seq_len, final_norm):
    t = x2d.shape[0]
    b = conv_prev.shape[0]
    tiles_per_seq = seq_len // tm
    const = lambda m, n: pl.BlockSpec((m, n), lambda i: (0, 0), pipeline_mode=pl.Buffered(1))
    return pl.pallas_call(
        functools.partial(_post_kernel, tiles_per_seq=tiles_per_seq, tm=tm, final_norm=final_norm),
        grid=(t // tm,),
        in_specs=[
            pl.BlockSpec((tm, D_MODEL), lambda i: (i, 0)),
            pl.BlockSpec((tm, GLA_V), lambda i: (i, 0)),
            pl.BlockSpec((tm, RWKV_C), lambda i: (i, 0)),
            const(D_MODEL, D_MODEL),
            const(1, D_MODEL),
            const(D_MODEL, D_FF),
            const(D_MODEL, D_FF),
            const(CONV_W, D_FF),
            const(1, D_FF),
            const(D_FF, D_MODEL),
            pl.BlockSpec((1, CONV_W - 1, D_FF), lambda i: (i // tiles_per_seq, 0, 0)),
            const(1, D_MODEL),
        ],
        out_specs=[
            pl.BlockSpec((tm, D_MODEL), lambda i: (i, 0)),
            pl.BlockSpec((1, CONV_W - 1, D_FF), lambda i: (i // tiles_per_seq, 0, 0)),
        ],
        out_shape=[
            jax.ShapeDtypeStruct((t, D_MODEL), F32),
            jax.ShapeDtypeStruct((b, CONV_W - 1, D_FF), F32),
        ],
        scratch_shapes=[
            pltpu.VMEM((tm, D_FF), BF16),
            pltpu.VMEM((CONV_W - 1, D_FF), F32),
        ],
        compiler_params=pltpu.CompilerParams(
            dimension_semantics=("arbitrary",), vmem_limit_bytes=VMEM_LIMIT_BYTES),
        name="post",
    )(x2d, o_gla, y_rwkv, prm["w_out"], prm["norm_ffn"], prm["w_up_u"], prm["w_up_g"],
      prm["conv_w"], prm["conv_b"], prm["w_down"], conv_prev, prm["norm_final"])


def _layer_params(l, P):
    w_in = P["w_in"][l]
    zpad = lambda a, rows_before, rows_total: jnp.zeros((rows_total, a.shape[1]), a.dtype).at[
        rows_before:rows_before + a.shape[0]].set(a)
    w_gla = jnp.concatenate(
        [w_in[:, :GLA_COLS], jnp.zeros((D_MODEL, GLA_SLAB - GLA_COLS), w_in.dtype)], axis=1)
    r = lambda a: a.reshape(1, -1)
    return dict(
        norm_mix=r(P["norm_mix"][l]),
        w_gla=w_gla.astype(BF16),
        w_rwkv=w_in[:, GLA_COLS:].astype(BF16),
        gate_w2p=zpad(P["gla_gate_w2"][l], 0, LANES).astype(BF16),
        gate_b=r(P["gla_gate_b"][l]),
        gla_norm_w=r(P["gla_norm_w"][l]),
        mu=r(P["rwkv_mu"][l]),
        w0=r(P["rwkv_w0"][l]),
        w2p=zpad(P["rwkv_w2"][l], 0, LANES).astype(BF16),
        a0=r(P["rwkv_a0"][l]),
        a2p=zpad(P["rwkv_a2"][l], RWKV_W_LR, LANES).astype(BF16),
        g2=P["rwkv_g2"][l].astype(BF16),
        k_k=r(P["rwkv_k_k"][l]), k_a=r(P["rwkv_k_a"][l]), r_k=r(P["rwkv_r_k"][l]),
        ln_w=r(P["rwkv_ln_w"][l]), ln_b=r(P["rwkv_ln_b"][l]),
        w_out=P["w_out"][l].astype(BF16),
        norm_ffn=r(P["norm_ffn"][l]),
        w_up_u=P["ffn_w_up"][l][:, :D_FF].astype(BF16),
        w_up_g=P["ffn_w_up"][l][:, D_FF:].astype(BF16),
        conv_w=P["ffn_conv_w"][l], conv_b=r(P["ffn_conv_b"][l]),
        w_down=P["ffn_w_down"][l].astype(BF16),
        norm_final=r(P["norm_final"]),
    )


def _gla_state_in(s):
    st = jnp.swapaxes(s, -1, -2)
    z = jnp.zeros_like(st)
    par = (jnp.arange(GLA_HEADS) % 2).reshape(1, GLA_HEADS, 1, 1)
    return jnp.where(par == 0, jnp.concatenate([st, z], -1), jnp.concatenate([z, st], -1))


def _gla_state_out(st):
    lo, hi = st[..., :GLA_DK], st[..., GLA_DK:]
    par = (jnp.arange(GLA_HEADS) % 2).reshape(1, GLA_HEADS, 1, 1)
    return jnp.swapaxes(jnp.where(par == 0, lo, hi), -1, -2)


def _rwkv_state_in(s):
    b = s.shape[0]
    s = s.reshape(b, RWKV_PAIRS, 2, RWKV_N, RWKV_N)
    z = jnp.zeros_like(s[:, :, 0])
    top = jnp.concatenate([s[:, :, 0], z], -1)
    bot = jnp.concatenate([z, s[:, :, 1]], -1)
    return jnp.concatenate([top, bot], -2)


def _rwkv_state_out(sb):
    b = sb.shape[0]
    s0 = sb[:, :, :RWKV_N, :RWKV_N]
    s1 = sb[:, :, RWKV_N:, RWKV_N:]
    return jnp.stack([s0, s1], axis=2).reshape(b, RWKV_HEADS, RWKV_N, RWKV_N)


def _tiles(seq_len):
    tm = min(512, seq_len)
    tl = min(256, seq_len)
    return tm, tl


def _trunk(x, s_gla, s_rwkv, s_shift, c_conv, layers):
    b, l, _ = x.shape
    tm, tl = _tiles(l)
    x2d = x.reshape(b * l, D_MODEL)
    new_gla, new_rwkv, new_shift, new_conv = [], [], [], []
    for li, prm in enumerate(layers):
        p_gla, p_rwkv = _inproj(x2d, prm["norm_mix"], prm["w_gla"], prm["w_rwkv"], tm)
        o_gla, st = _gla(p_gla.reshape(b, l, GLA_SLAB), prm["gate_w2p"], prm["gate_b"],
                         prm["gla_norm_w"], _gla_state_in(s_gla[li]), tl)
        y_rwkv, sb, sh = _rwkv(p_rwkv.reshape(b, l, RWKV_COLS), s_shift[li][:, None, :], prm,
                               _rwkv_state_in(s_rwkv[li]), tl)
        x2d, cc = _post(x2d, o_gla.reshape(b * l, GLA_V), y_rwkv.reshape(b * l, RWKV_C), prm,
                        c_conv[li], tm, l, li == len(layers) - 1)
        new_gla.append(_gla_state_out(st))
        new_rwkv.append(_rwkv_state_out(sb))
        new_shift.append(sh[:, 0, :])
        new_conv.append(cc)
    return (x2d.reshape(b, l, D_MODEL), jnp.stack(new_gla), jnp.stack(new_rwkv),
            jnp.stack(new_shift), jnp.stack(new_conv))


def kernel(x_prompt, x_sample, state_gla, state_rwkv, state_rwkv_shift, cache_ffn_conv, norm_mix, w_in, gla_gate_w2, gla_gate_b, gla_norm_w, rwkv_mu, rwkv_w0, rwkv_w2, rwkv_a0, rwkv_a2, rwkv_g2, rwkv_k_k, rwkv_k_a, rwkv_r_k, rwkv_ln_w, rwkv_ln_b, w_out, norm_ffn, ffn_w_up, ffn_conv_w, ffn_conv_b, ffn_w_down, norm_final):
    P = dict(norm_mix=norm_mix, w_in=w_in, gla_gate_w2=gla_gate_w2, gla_gate_b=gla_gate_b,
             gla_norm_w=gla_norm_w, rwkv_mu=rwkv_mu, rwkv_w0=rwkv_w0, rwkv_w2=rwkv_w2,
             rwkv_a0=rwkv_a0, rwkv_a2=rwkv_a2, rwkv_g2=rwkv_g2, rwkv_k_k=rwkv_k_k,
             rwkv_k_a=rwkv_k_a, rwkv_r_k=rwkv_r_k.reshape(DEPTH, RWKV_C), rwkv_ln_w=rwkv_ln_w,
             rwkv_ln_b=rwkv_ln_b, w_out=w_out, norm_ffn=norm_ffn, ffn_w_up=ffn_w_up,
             ffn_conv_w=ffn_conv_w, ffn_conv_b=ffn_conv_b, ffn_w_down=ffn_w_down,
             norm_final=norm_final)
    layers = [_layer_params(l, P) for l in range(DEPTH)]
    nb = x_prompt.shape[0]
    g0 = jnp.zeros((DEPTH, nb) + state_gla.shape[2:], state_gla.dtype)
    r0 = jnp.zeros((DEPTH, nb) + state_rwkv.shape[2:], state_rwkv.dtype)
    sh0 = jnp.zeros((DEPTH, nb) + state_rwkv_shift.shape[2:], state_rwkv_shift.dtype)
    c0 = jnp.zeros((DEPTH, nb) + cache_ffn_conv.shape[2:], cache_ffn_conv.dtype)
    y_p, gla_p, rwkv_p, shift_p, conv_p = _trunk(x_prompt, g0, r0, sh0, c0, layers)
    y_s, gla_s, rwkv_s, shift_s, conv_s = _trunk(
        x_sample, state_gla, state_rwkv, state_rwkv_shift, cache_ffn_conv, layers)
    return (y_p, y_s, gla_p, rwkv_p, shift_p, conv_p, gla_s, rwkv_s, shift_s, conv_s)
```

```python
import functools
import math

import jax
import jax.numpy as jnp
from jax import lax
from jax.experimental import pallas as pl
from jax.experimental.pallas import tpu as pltpu

F32 = jnp.float32
BF16 = jnp.bfloat16

D_MODEL = 1024
DEPTH = 2
CHUNK = 64
NORM_EPS = 1e-6
GLA_HEADS = 4
GLA_DV = 128
GLA_DK = 64
GLA_K = GLA_HEADS * GLA_DK
GLA_V = GLA_HEADS * GLA_DV
GLA_LR = 16
GLA_GATE_NORMALIZER = 16.0
RWKV_N = 64
RWKV_C = 512
RWKV_HEADS = 8
RWKV_PAIRS = RWKV_HEADS // 2
RWKV_W_LR = 64
RWKV_A_LR = 64
RWKV_G_LR = 128
RWKV_GN_EPS = 64e-5
GLA_COLS = 2 * GLA_K + 2 * GLA_V + GLA_LR
RWKV_COLS = 3 * RWKV_C + RWKV_W_LR + RWKV_A_LR + RWKV_G_LR
D_FF = 2816
CONV_W = 3

LANES = 128
MXU_TILE = 256
GLA_SLAB = 2 * GLA_K + 2 * GLA_V + LANES
VMEM_LIMIT_BYTES = 56 * 1024 * 1024
RWKV_SEQS_PER_STEP = 4
RWKV_STREAM_LAG = 8

_GQ, _GK, _GV, _GG, _GLR = 0, GLA_K, 2 * GLA_K, 2 * GLA_K + GLA_V, 2 * GLA_K + 2 * GLA_V
_RR, _RK, _RV, _RWA, _RG = 0, RWKV_C, 2 * RWKV_C, 3 * RWKV_C, 3 * RWKV_C + RWKV_W_LR + RWKV_A_LR


def _bdot(a, b):
    return jnp.dot(a.astype(BF16), b.astype(BF16), preferred_element_type=F32)


def _bdot_nt(a, b):
    return lax.dot_general(a.astype(BF16), b.astype(BF16), (((1,), (1,)), ((), ())),
                           preferred_element_type=F32)


def _bdot_tn(a, b):
    return lax.dot_general(a.astype(BF16), b.astype(BF16), (((0,), (0,)), ((), ())),
                           preferred_element_type=F32)


def _split3(x):
    hi = x.astype(BF16)
    r1 = x - hi.astype(F32)
    mid = r1.astype(BF16)
    lo = (r1 - mid.astype(F32)).astype(BF16)
    return hi, mid, lo


def _exact_dot(m_bf16, x, pieces=3):
    d = lambda p: jnp.dot(m_bf16, p, preferred_element_type=F32)
    if pieces == 2:
        hi = x.astype(BF16)
        return d(hi) + d((x - hi.astype(F32)).astype(BF16))
    hi, mid, lo = _split3(x)
    return d(hi) + d(mid) + d(lo)


def _head_sum(x):
    r = lax.broadcasted_iota(jnp.int32, (MXU_TILE, MXU_TILE), 0)
    c = lax.broadcasted_iota(jnp.int32, (MXU_TILE, MXU_TILE), 1)
    ones_blk = ((r // RWKV_N) == (c // RWKV_N)).astype(BF16)
    xb = x.astype(BF16)
    out = [jnp.dot(xb[:, g * MXU_TILE:(g + 1) * MXU_TILE], ones_blk, preferred_element_type=F32)
           for g in range(x.shape[1] // MXU_TILE)]
    return jnp.concatenate(out, axis=1)


def _tril_incl(n):
    r = lax.broadcasted_iota(jnp.int32, (n, n), 0)
    c = lax.broadcasted_iota(jnp.int32, (n, n), 1)
    return (c <= r)


def _rmsnorm(x, g):
    return x * lax.rsqrt(jnp.mean(x * x, axis=-1, keepdims=True) + NORM_EPS) * g


def _interleave(streams, lag):
    active, waiting, rnd = [], list(streams), 0
    while active or waiting:
        if waiting and rnd % lag == 0:
            active.append(waiting.pop(0))
        for g in list(active):
            if next(g, _DONE) is _DONE:
                active.remove(g)
        rnd += 1


_DONE = object()


def _shift_rows(x, k, fill_rows):
    y = pltpu.roll(x, k, 0)
    row = lax.broadcasted_iota(jnp.int32, x.shape, 0)
    for i, f in enumerate(fill_rows):
        y = jnp.where(row == i, f, y)
    return y


def _inproj_kernel(x_ref, g_ref, wg_ref, wr_ref, og_ref, or_ref):
    h = _rmsnorm(x_ref[...], g_ref[...]).astype(BF16)
    og_ref[...] = jnp.dot(h, wg_ref[...], preferred_element_type=F32)
    or_ref[...] = jnp.dot(h, wr_ref[...], preferred_element_type=F32)


def _inproj(x2d, g, w_gla, w_rwkv, tm):
    t = x2d.shape[0]
    return pl.pallas_call(
        _inproj_kernel,
        grid=(t // tm,),
        in_specs=[
            pl.BlockSpec((tm, D_MODEL), lambda i: (i, 0)),
            pl.BlockSpec((1, D_MODEL), lambda i: (0, 0)),
            pl.BlockSpec((D_MODEL, GLA_SLAB), lambda i: (0, 0)),
            pl.BlockSpec((D_MODEL, RWKV_COLS), lambda i: (0, 0)),
        ],
        out_specs=[
            pl.BlockSpec((tm, GLA_SLAB), lambda i: (i, 0)),
            pl.BlockSpec((tm, RWKV_COLS), lambda i: (i, 0)),
        ],
        out_shape=[
            jax.ShapeDtypeStruct((t, GLA_SLAB), F32),
            jax.ShapeDtypeStruct((t, RWKV_COLS), F32),
        ],
        compiler_params=pltpu.CompilerParams(
            dimension_semantics=("arbitrary",), vmem_limit_bytes=VMEM_LIMIT_BYTES),
        name="inproj",
    )(x2d, g, w_gla, w_rwkv)


def _gla_kernel(p_ref, w2_ref, b_ref, nw_ref, s0_ref, o_ref, sout_ref, st_ref, *, n_chunks):
    ti = pl.program_id(1)

    @pl.when(ti == 0)
    def _():
        st_ref[...] = s0_ref[0]

    tl = n_chunks * CHUNK
    ri = lax.broadcasted_iota(jnp.int32, (tl, tl), 0)
    ci = lax.broadcasted_iota(jnp.int32, (tl, tl), 1)
    same_chunk = (ri // CHUNK) == (ci // CHUNK)
    causal = same_chunk & (ci <= ri)
    lane = lax.broadcasted_iota(jnp.int32, (tl, LANES), 1)
    half = [lane < GLA_DK, lane >= GLA_DK]
    nw = nw_ref[...]

    logit = _bdot(p_ref[0, :, _GLR:_GLR + LANES], w2_ref[...]) + b_ref[...]
    la = jax.nn.log_sigmoid(logit) / GLA_GATE_NORMALIZER
    cum = _exact_dot(causal.astype(BF16), la)
    tot = _exact_dot(same_chunk.astype(BF16), la)
    q = p_ref[0, :, _GQ:_GQ + GLA_K] * (GLA_DK ** -0.5)
    k = p_ref[0, :, _GK:_GK + GLA_K]
    q_e = q * jnp.exp(cum)
    k_e = k * jnp.exp(-cum)
    k_hat = k * jnp.exp(tot - cum)
    dec = jnp.exp(tot)
    heads = range(GLA_HEADS)
    chunks = range(n_chunks)
    crow = lambda x, c: x[c * CHUNK:(c + 1) * CHUNK]
    pair = lambda x, h: x[:, (h // 2) * LANES:(h // 2 + 1) * LANES]
    v = [p_ref[0, :, _GV + h * GLA_DV:_GV + (h + 1) * GLA_DV] for h in heads]

    sc = [_bdot_nt(jnp.where(half[h % 2], pair(q_e, h), 0.0), pair(k_e, h)) for h in heads]
    o_intra = [_bdot(jnp.where(causal, sc[h], 0.0), v[h]) for h in heads]

    kh_m = [jnp.where(half[h % 2], pair(k_hat, h), 0.0) for h in heads]
    upd = [[_bdot_tn(crow(v[h], c), crow(kh_m[h], c)) for c in chunks] for h in heads]
    st_in = []
    for h in heads:
        st = st_ref[h]
        per_chunk = []
        for c in chunks:
            per_chunk.append(st)
            st = st * pair(dec, h)[c * CHUNK:c * CHUNK + 1] + upd[h][c]
        st_ref[h] = st
        st_in.append(per_chunk)
    o_inter = [jnp.concatenate([_bdot_nt(crow(pair(q_e, h), c), st_in[h][c]) for c in chunks], axis=0)
               for h in heads]

    for h in heads:
        o = o_intra[h] + o_inter[h]
        gg = p_ref[0, :, _GG + h * GLA_DV:_GG + (h + 1) * GLA_DV]
        o = o * lax.rsqrt(jnp.mean(o * o, axis=-1, keepdims=True) + NORM_EPS) * nw
        o = o * (gg * jax.nn.sigmoid(gg))
        o_ref[0, :, h * GLA_DV:(h + 1) * GLA_DV] = o.astype(o_ref.dtype)
    sout_ref[0] = st_ref[...]


def _gla(p_gla, w2p, gate_b, norm_w, st0, tl):
    b, l, _ = p_gla.shape
    return pl.pallas_call(
        functools.partial(_gla_kernel, n_chunks=tl // CHUNK),
        grid=(b, l // tl),
        in_specs=[
            pl.BlockSpec((1, tl, GLA_SLAB), lambda i, j: (i, j, 0)),
            pl.BlockSpec((LANES, GLA_K), lambda i, j: (0, 0)),
            pl.BlockSpec((1, GLA_K), lambda i, j: (0, 0)),
            pl.BlockSpec((1, GLA_DV), lambda i, j: (0, 0)),
            pl.BlockSpec((1, GLA_HEADS, GLA_DV, LANES), lambda i, j: (i, 0, 0, 0)),
        ],
        out_specs=[
            pl.BlockSpec((1, tl, GLA_V), lambda i, j: (i, j, 0)),
            pl.BlockSpec((1, GLA_HEADS, GLA_DV, LANES), lambda i, j: (i, 0, 0, 0)),
        ],
        out_shape=[
            jax.ShapeDtypeStruct((b, l, GLA_V), BF16),
            jax.ShapeDtypeStruct((b, GLA_HEADS, GLA_DV, LANES), F32),
        ],
        scratch_shapes=[pltpu.VMEM((GLA_HEADS, GLA_DV, LANES), F32)],
        compiler_params=pltpu.CompilerParams(
            dimension_semantics=("arbitrary", "arbitrary"), vmem_limit_bytes=VMEM_LIMIT_BYTES),
        name="gla",
    )(p_gla, w2p, gate_b, norm_w, st0)


def _rwkv_kernel(p_ref, sh_ref, mu_ref, w0_ref, w2_ref, a0_ref, a2_ref, g2_ref, kk_ref, ka_ref,
                 rk_ref, lnw_ref, lnb_ref, s0_ref,
                 y_ref, sout_ref, shout_ref,
                 s_ref, prev_ref, *, n_chunks, tl, nb, lag):
    ti = pl.program_id(1)

    @pl.when(ti == 0)
    def _():
        s_ref[...] = s0_ref[...]
        prev_ref[...] = sh_ref[...]

    seg_sum = _head_sum
    tr = lax.broadcasted_iota(jnp.int32, (tl, tl), 0)
    tc = lax.broadcasted_iota(jnp.int32, (tl, tl), 1)
    chunk_tril = (((tr // CHUNK) == (tc // CHUNK)) & (tc <= tr)).astype(BF16)

    trow = lax.broadcasted_iota(jnp.int32, (CHUNK, LANES), 0)
    tcol = lax.broadcasted_iota(jnp.int32, (CHUNK, LANES), 1) & (RWKV_N - 1)
    strict = tcol < trow
    incl = tcol <= trow
    eye = (tcol == trow).astype(F32)
    ri = lax.broadcasted_iota(jnp.int32, (LANES, LANES), 0)
    ci = lax.broadcasted_iota(jnp.int32, (LANES, LANES), 1)
    same_head = (ri >> 6) == (ci >> 6)
    zeros_bd = jnp.zeros((LANES, LANES), BF16)

    def blockdiag(x):
        xb = x.astype(BF16)
        return jnp.where(same_head, jnp.concatenate([xb, xb], axis=0), zeros_bd)

    def stream(bi):
        p = p_ref[bi]
        prev = _shift_rows(p, 1, [prev_ref[bi]])
        prev_ref[bi] = p[tl - 1:tl, :]
        shout_ref[bi] = p[tl - 1:tl, :]
        xs = p + mu_ref[...] * (prev - p)
        xr = xs[:, _RR:_RR + RWKV_C]
        xk = xs[:, _RK:_RK + RWKV_C]
        xv = xs[:, _RV:_RV + RWKV_C]
        xwa = xs[:, _RWA:_RWA + LANES]
        xg = xs[:, _RG:_RG + RWKV_G_LR]
        yield
        w_raw = w0_ref[...] + _bdot(jnp.tanh(xwa), w2_ref[...])
        w_log = -(math.exp(-0.5)) * jax.nn.sigmoid(w_raw)
        a = jax.nn.sigmoid(a0_ref[...] + _bdot(xwa, a2_ref[...]))
        g = _bdot(jax.nn.sigmoid(xg), g2_ref[...])
        yield
        cl_tile = _exact_dot(chunk_tril, w_log, pieces=2)
        kk = xk * kk_ref[...]
        kk = kk * lax.rsqrt(jnp.maximum(seg_sum(kk * kk), 1e-24))
        yield
        kr = xk * (1.0 + (a - 1.0) * ka_ref[...])
        bonus = seg_sum(xr * kr * rk_ref[...]) * xv
        a_vec = -kk
        b_vec = kk * a
        yield

        units = [(slice(c * CHUNK, (c + 1) * CHUNK), slice(pr * LANES, (pr + 1) * LANES))
                 for c in range(n_chunks) for pr in range(RWKV_PAIRS)]
        n = range(len(units))
        a_t, r_t, b_t, k_t, b_h, k_h, v, dec = [], [], [], [], [], [], [], []
        for ui, (rows, cols) in enumerate(units):
            lw, cl = w_log[rows, cols], cl_tile[rows, cols]
            dec_row = jnp.exp(cl[CHUNK - 1:CHUNK, :])
            e_neg = jnp.exp(-cl)
            e_end = dec_row * e_neg
            av, bv, kv = a_vec[rows, cols], b_vec[rows, cols], kr[rows, cols]
            a_t.append(av * jnp.exp(cl - lw))
            r_t.append(xr[rows, cols] * jnp.exp(cl))
            b_t.append(bv * e_neg)
            k_t.append(kv * e_neg)
            b_h.append(bv * e_end)
            k_h.append(kv * e_end)
            v.append(xv[rows, cols])
            dec.append(dec_row)
            if ui % RWKV_PAIRS == RWKV_PAIRS - 1:
                yield
        gm = [_bdot_nt(jnp.concatenate([a_t[i], r_t[i]], axis=0),
                       jnp.concatenate([blockdiag(b_t[i]), blockdiag(k_t[i])], axis=0)) for i in n]
        yield
        a_ab = [jnp.where(strict, gm[i][:CHUNK, :LANES], 0.0) for i in n]
        a_ak = [jnp.where(strict, gm[i][:CHUNK, LANES:], 0.0) for i in n]
        r_b = [jnp.where(incl, gm[i][CHUNK:, :LANES], 0.0) for i in n]
        r_k = [jnp.where(incl, gm[i][CHUNK:, LANES:], 0.0) for i in n]
        inv = [eye + a_ab[i] for i in n]
        pw = a_ab
        for _ in range(5):
            pw = [_bdot(pw[i], blockdiag(pw[i])) for i in n]
            yield
            inv = [inv[i] + _bdot(pw[i], blockdiag(inv[i])) for i in n]
            yield
        v_bd = [blockdiag(v[i]) for i in n]
        akv = [_bdot(a_ak[i], v_bd[i]) for i in n]
        yield
        w = [_bdot(inv[i], jnp.concatenate([blockdiag(a_t[i]), blockdiag(akv[i])], axis=1))
             for i in n]
        a_bar = [w[i][:, :LANES] for i in n]
        u0 = [w[i][:, LANES:] for i in n]
        yield
        rbk = [jnp.concatenate([r_b[i], r_k[i]], axis=1) for i in n]
        bk_h = [jnp.concatenate([b_h[i], k_h[i]], axis=0) for i in n]
        pairs = range(RWKV_PAIRS)
        y_chunks = []
        for c in range(n_chunks):
            ids = [c * RWKV_PAIRS + pr for pr in pairs]
            s = [s_ref[bi, pr] for pr in pairs]
            u = [_bdot_nt(a_bar[i], s[pr]) + u0[i] for pr, i in zip(pairs, ids)]
            ys = [_bdot_nt(r_t[i], s[pr]) for pr, i in zip(pairs, ids)]
            yield
            upd = [_bdot_tn(jnp.concatenate([u[pr], v[i]], axis=0), bk_h[i])
                   for pr, i in zip(pairs, ids)]
            y_pairs = [ys[pr] + _bdot(rbk[i], jnp.concatenate([blockdiag(u[pr]), v_bd[i]], axis=0))
                       for pr, i in zip(pairs, ids)]
            for pr, i in zip(pairs, ids):
                s_ref[bi, pr] = s[pr] * dec[i] + jnp.where(same_head, upd[pr], 0.0)
            y_chunks.append(jnp.concatenate(y_pairs, axis=1))
            yield
        sout_ref[bi] = s_ref[bi]

        y = jnp.concatenate(y_chunks, axis=0)
        mean = seg_sum(y) * (1.0 / RWKV_N)
        yc = y - mean
        yield
        var = seg_sum(yc * yc) * (1.0 / RWKV_N)
        y = yc * lax.rsqrt(var + RWKV_GN_EPS) * lnw_ref[...] + lnb_ref[...]
        y_ref[bi] = ((y + bonus) * g).astype(y_ref.dtype)

    _interleave([stream(bi) for bi in range(nb)], lag)


def _rwkv(p_rwkv, shift_prev, prm, s0, tl):
    b, l, _ = p_rwkv.shape
    nb = RWKV_SEQS_PER_STEP
    row = lambda n: pl.BlockSpec((1, n), lambda i, j: (0, 0))
    mat = lambda m, n: pl.BlockSpec((m, n), lambda i, j: (0, 0))
    return pl.pallas_call(
        functools.partial(_rwkv_kernel, n_chunks=tl // CHUNK, tl=tl, nb=nb, lag=RWKV_STREAM_LAG),
        grid=(b // nb, l // tl),
        in_specs=[
            pl.BlockSpec((nb, tl, RWKV_COLS), lambda i, j: (i, j, 0)),
            pl.BlockSpec((nb, 1, RWKV_COLS), lambda i, j: (i, 0, 0)),
            row(RWKV_COLS),
            row(RWKV_C), mat(LANES, RWKV_C), row(RWKV_C), mat(LANES, RWKV_C), mat(RWKV_G_LR, RWKV_C),
            row(RWKV_C), row(RWKV_C), row(RWKV_C), row(RWKV_C), row(RWKV_C),
            pl.BlockSpec((nb, RWKV_PAIRS, LANES, LANES), lambda i, j: (i, 0, 0, 0)),
        ],
        out_specs=[
            pl.BlockSpec((nb, tl, RWKV_C), lambda i, j: (i, j, 0)),
            pl.BlockSpec((nb, RWKV_PAIRS, LANES, LANES), lambda i, j: (i, 0, 0, 0)),
            pl.BlockSpec((nb, 1, RWKV_COLS), lambda i, j: (i, 0, 0)),
        ],
        out_shape=[
            jax.ShapeDtypeStruct((b, l, RWKV_C), BF16),
            jax.ShapeDtypeStruct((b, RWKV_PAIRS, LANES, LANES), F32),
            jax.ShapeDtypeStruct((b, 1, RWKV_COLS), F32),
        ],
        scratch_shapes=[
            pltpu.VMEM((nb, RWKV_PAIRS, LANES, LANES), F32),
            pltpu.VMEM((nb, 1, RWKV_COLS), F32),
        ],
        compiler_params=pltpu.CompilerParams(
            dimension_semantics=("arbitrary", "arbitrary"), vmem_limit_bytes=VMEM_LIMIT_BYTES),
        name="rwkv",
    )(p_rwkv, shift_prev, prm["mu"], prm["w0"], prm["w2p"], prm["a0"], prm["a2p"], prm["g2"],
      prm["k_k"], prm["k_a"], prm["r_k"], prm["ln_w"], prm["ln_b"], s0)


def _post_kernel(x_ref, og_ref, yr_ref, wo_ref, nf_ref, wu_ref, wg_ref, cw_ref, cb_ref, wd_ref,
                 cprev_ref, nfin_ref, out_ref, cnew_ref, act_ref, carry_ref,
                 *, tiles_per_seq, tm, final_norm):
    i = pl.program_id(0)
    wo = wo_ref[...]
    mix = (jnp.dot(og_ref[...], wo[:GLA_V], preferred_element_type=F32)
           + jnp.dot(yr_ref[...], wo[GLA_V:], preferred_element_type=F32))
    x1 = x_ref[...] + mix
    out_ref[...] = x1
    h = _rmsnorm(x1, nf_ref[...]).astype(BF16)

    @pl.when(i % tiles_per_seq == 0)
    def _():
        carry_ref[...] = cprev_ref[0]

    for j in range(D_FF // MXU_TILE):
        cols = slice(j * MXU_TILE, (j + 1) * MXU_TILE)
        u = jnp.dot(h, wu_ref[:, cols], preferred_element_type=F32)
        gate = jnp.dot(h, wg_ref[:, cols], preferred_element_type=F32)
        p0, p1 = carry_ref[0:1, cols], carry_ref[1:2, cols]
        uc = (cb_ref[:, cols] + cw_ref[0:1, cols] * _shift_rows(u, 2, [p0, p1])
              + cw_ref[1:2, cols] * _shift_rows(u, 1, [p1]) + cw_ref[2:3, cols] * u)
        new_prev = u[tm - 2:tm, :]
        carry_ref[:, cols] = new_prev
        cnew_ref[0, :, cols] = new_prev
        act = 0.5 * uc * (1.0 + lax.erf(uc * (2.0 ** -0.5))) * gate
        act_ref[:, cols] = act.astype(BF16)

    x2 = out_ref[...] + jnp.dot(act_ref[...], wd_ref[...], preferred_element_type=F32)
    if final_norm:
        x2 = _rmsnorm(x2, nfin_ref[...])
    out_ref[...] = x2


def _post(x2d, o_gla, y_rwkv, prm, conv_prev, tm, seq_len, final_norm):
    t = x2d.shape[0]
    b = conv_prev.shape[0]
    tiles_per_seq = seq_len // tm
    const = lambda m, n: pl.BlockSpec((m, n), lambda i: (0, 0), pipeline_mode=pl.Buffered(1))
    return pl.pallas_call(
        functools.partial(_post_kernel, tiles_per_seq=tiles_per_seq, tm=tm, final_norm=final_norm),
        grid=(t // tm,),
        in_specs=[
            pl.BlockSpec((tm, D_MODEL), lambda i: (i, 0)),
            pl.BlockSpec((tm, GLA_V), lambda i: (i, 0)),
            pl.BlockSpec((tm, RWKV_C), lambda i: (i, 0)),
            const(D_MODEL, D_MODEL),
            const(1, D_MODEL),
            const(D_MODEL, D_FF),
            const(D_MODEL, D_FF),
            const(CONV_W, D_FF),
            const(1, D_FF),
            const(D_FF, D_MODEL),
            pl.BlockSpec((1, CONV_W - 1, D_FF), lambda i: (i // tiles_per_seq, 0, 0)),
            const(1, D_MODEL),
        ],
        out_specs=[
            pl.BlockSpec((tm, D_MODEL), lambda i: (i, 0)),
            pl.BlockSpec((1, CONV_W - 1, D_FF), lambda i: (i // tiles_per_seq, 0, 0)),
        ],
        out_shape=[
            jax.ShapeDtypeStruct((t, D_MODEL), F32),
            jax.ShapeDtypeStruct((b, CONV_W - 1, D_FF), F32),
        ],
        scratch_shapes=[
            pltpu.VMEM((tm, D_FF), BF16),
            pltpu.VMEM((CONV_W - 1, D_FF), F32),
        ],
        compiler_params=pltpu.CompilerParams(
            dimension_semantics=("arbitrary",), vmem_limit_bytes=VMEM_LIMIT_BYTES),
        name="post",
    )(x2d, o_gla, y_rwkv, prm["w_out"], prm["norm_ffn"], prm["w_up_u"], prm["w_up_g"],
      prm["conv_w"], prm["conv_b"], prm["w_down"], conv_prev, prm["norm_final"])


def _layer_params(l, P):
    w_in = P["w_in"][l]
    zpad = lambda a, rows_before, rows_total: jnp.zeros((rows_total, a.shape[1]), a.dtype).at[
        rows_before:rows_before + a.shape[0]].set(a)
    w_gla = jnp.concatenate(
        [w_in[:, :GLA_COLS], jnp.zeros((D_MODEL, GLA_SLAB - GLA_COLS), w_in.dtype)], axis=1)
    r = lambda a: a.reshape(1, -1)
    return dict(
        norm_mix=r(P["norm_mix"][l]),
        w_gla=w_gla.astype(BF16),
        w_rwkv=w_in[:, GLA_COLS:].astype(BF16),
        gate_w2p=zpad(P["gla_gate_w2"][l], 0, LANES).astype(BF16),
        gate_b=r(P["gla_gate_b"][l]),
        gla_norm_w=r(P["gla_norm_w"][l]),
        mu=r(P["rwkv_mu"][l]),
        w0=r(P["rwkv_w0"][l]),
        w2p=zpad(P["rwkv_w2"][l], 0, LANES).astype(BF16),
        a0=r(P["rwkv_a0"][l]),
        a2p=zpad(P["rwkv_a2"][l], RWKV_W_LR, LANES).astype(BF16),
        g2=P["rwkv_g2"][l].astype(BF16),
        k_k=r(P["rwkv_k_k"][l]), k_a=r(P["rwkv_k_a"][l]), r_k=r(P["rwkv_r_k"][l]),
        ln_w=r(P["rwkv_ln_w"][l]), ln_b=r(P["rwkv_ln_b"][l]),
        w_out=P["w_out"][l].astype(BF16),
        norm_ffn=r(P["norm_ffn"][l]),
        w_up_u=P["ffn_w_up"][l][:, :D_FF].astype(BF16),
        w_up_g=P["ffn_w_up"][l][:, D_FF:].astype(BF16),
        conv_w=P["ffn_conv_w"][l], conv_b=r(P["ffn_conv_b"][l]),
        w_down=P["ffn_w_down"][l].astype(BF16),
        norm_final=r(P["norm_final"]),
    )


def _gla_state_in(s):
    st = jnp.swapaxes(s, -1, -2)
    z = jnp.zeros_like(st)
    par = (jnp.arange(GLA_HEADS) % 2).reshape(1, GLA_HEADS, 1, 1)
    return jnp.where(par == 0, jnp.concatenate([st, z], -1), jnp.concatenate([z, st], -1))


def _gla_state_out(st):
    lo, hi = st[..., :GLA_DK], st[..., GLA_DK:]
    par = (jnp.arange(GLA_HEADS) % 2).reshape(1, GLA_HEADS, 1, 1)
    return jnp.swapaxes(jnp.where(par == 0, lo, hi), -1, -2)


def _rwkv_state_in(s):
    b = s.shape[0]
    s = s.reshape(b, RWKV_PAIRS, 2, RWKV_N, RWKV_N)
    z = jnp.zeros_like(s[:, :, 0])
    top = jnp.concatenate([s[:, :, 0], z], -1)
    bot = jnp.concatenate([z, s[:, :, 1]], -1)
    return jnp.concatenate([top, bot], -2)


def _rwkv_state_out(sb):
    b = sb.shape[0]
    s0 = sb[:, :, :RWKV_N, :RWKV_N]
    s1 = sb[:, :, RWKV_N:, RWKV_N:]
    return jnp.stack([s0, s1], axis=2).reshape(b, RWKV_HEADS, RWKV_N, RWKV_N)


def _tiles(seq_len):
    tm = min(512, seq_len)
    tl = min(256, seq_len)
    return tm, tl


def _trunk(x, s_gla, s_rwkv, s_shift, c_conv, layers):
    b, l, _ = x.shape
    tm, tl = _tiles(l)
    x2d = x.reshape(b * l, D_MODEL)
    new_gla, new_rwkv, new_shift, new_conv = [], [], [], []
    for li, prm in enumerate(layers):
        p_gla, p_rwkv = _inproj(x2d, prm["norm_mix"], prm["w_gla"], prm["w_rwkv"], tm)
        o_gla, st = _gla(p_gla.reshape(b, l, GLA_SLAB), prm["gate_w2p"], prm["gate_b"],
                         prm["gla_norm_w"], _gla_state_in(s_gla[li]), tl)
        y_rwkv, sb, sh = _rwkv(p_rwkv.reshape(b, l, RWKV_COLS), s_shift[li][:, None, :], prm,
                               _rwkv_state_in(s_rwkv[li]), tl)
        x2d, cc = _post(x2d, o_gla.reshape(b * l, GLA_V), y_rwkv.reshape(b * l, RWKV_C), prm,
                        c_conv[li], tm, l, li == len(layers) - 1)
        new_gla.append(_gla_state_out(st))
        new_rwkv.append(_rwkv_state_out(sb))
        new_shift.append(sh[:, 0, :])
        new_conv.append(cc)
    return (x2d.reshape(b, l, D_MODEL), jnp.stack(new_gla), jnp.stack(new_rwkv),
            jnp.stack(new_shift), jnp.stack(new_conv))


def kernel(x_prompt, x_sample, state_gla, state_rwkv, state_rwkv_shift, cache_ffn_conv, norm_mix, w_in, gla_gate_w2, gla_gate_b, gla_norm_w, rwkv_mu, rwkv_w0, rwkv_w2, rwkv_a0, rwkv_a2, rwkv_g2, rwkv_k_k, rwkv_k_a, rwkv_r_k, rwkv_ln_w, rwkv_ln_b, w_out, norm_ffn, ffn_w_up, ffn_conv_w, ffn_conv_b, ffn_w_down, norm_final):
    P = dict(norm_mix=norm_mix, w_in=w_in, gla_gate_w2=gla_gate_w2, gla_gate_b=gla_gate_b,
             gla_norm_w=gla_norm_w, rwkv_mu=rwkv_mu, rwkv_w0=rwkv_w0, rwkv_w2=rwkv_w2,
             rwkv_a0=rwkv_a0, rwkv_a2=rwkv_a2, rwkv_g2=rwkv_g2, rwkv_k_k=rwkv_k_k,
             rwkv_k_a=rwkv_k_a, rwkv_r_k=rwkv_r_k.reshape(DEPTH, RWKV_C), rwkv_ln_w=rwkv_ln_w,
             rwkv_ln_b=rwkv_ln_b, w_out=w_out, norm_ffn=norm_ffn, ffn_w_up=ffn_w_up,
             ffn_conv_w=ffn_conv_w, ffn_conv_b=ffn_conv_b, ffn_w_down=ffn_w_down,
             norm_final=norm_final)
    layers = [_layer_params(l, P) for l in range(DEPTH)]
    nb = x_prompt.shape[0]
    g0 = jnp.zeros((DEPTH, nb) + state_gla.shape[2:], state_gla.dtype)
    r0 = jnp.zeros((DEPTH, nb) + state_rwkv.shape[2:], state_rwkv.dtype)
    sh0 = jnp.zeros((DEPTH, nb) + state_rwkv_shift.shape[2:], state_rwkv_shift.dtype)
    c0 = jnp.zeros((DEPTH, nb) + cache_ffn_conv.shape[2:], cache_ffn_conv.dtype)
    y_p, gla_p, rwkv_p, shift_p, conv_p = _trunk(x_prompt, g0, r0, sh0, c0, layers)
    y_s, gla_s, rwkv_s, shift_s, conv_s = _trunk(
        x_sample, state_gla, state_rwkv, state_rwkv_shift, cache_ffn_conv, layers)
    return (y_p, y_s, gla_p, rwkv_p, shift_p, conv_p, gla_s, rwkv_s, shift_s, conv_s)
```

```python
import functools
import math

import jax
import jax.numpy as jnp
from jax import lax
from jax.experimental import pallas as pl
from jax.experimental.pallas import tpu as pltpu

F32 = jnp.float32
BF16 = jnp.bfloat16

D_MODEL = 1024
DEPTH = 2
CHUNK = 64
NORM_EPS = 1e-6
GLA_HEADS = 4
GLA_DV = 128
GLA_DK = 64
GLA_K = GLA_HEADS * GLA_DK
GLA_V = GLA_HEADS * GLA_DV
GLA_LR = 16
GLA_GATE_NORMALIZER = 16.0
RWKV_N = 64
RWKV_C = 512
RWKV_HEADS = 8
RWKV_PAIRS = RWKV_HEADS // 2
RWKV_W_LR = 64
RWKV_A_LR = 64
RWKV_G_LR = 128
RWKV_GN_EPS = 64e-5
GLA_COLS = 2 * GLA_K + 2 * GLA_V + GLA_LR
RWKV_COLS = 3 * RWKV_C + RWKV_W_LR + RWKV_A_LR + RWKV_G_LR
D_FF = 2816
CONV_W = 3

LANES = 128
MXU_TILE = 256
GLA_SLAB = 2 * GLA_K + 2 * GLA_V + LANES
VMEM_LIMIT_BYTES = 56 * 1024 * 1024
RWKV_SEQS_PER_STEP = 4
RWKV_STREAM_LAG = 8
GLA_SEQS_PER_STEP = 4
GLA_STREAM_LAG = 3

_GQ, _GK, _GV, _GG, _GLR = 0, GLA_K, 2 * GLA_K, 2 * GLA_K + GLA_V, 2 * GLA_K + 2 * GLA_V
_RR, _RK, _RV, _RWA, _RG = 0, RWKV_C, 2 * RWKV_C, 3 * RWKV_C, 3 * RWKV_C + RWKV_W_LR + RWKV_A_LR


def _bdot(a, b):
    return jnp.dot(a.astype(BF16), b.astype(BF16), preferred_element_type=F32)


def _bdot_nt(a, b):
    return lax.dot_general(a.astype(BF16), b.astype(BF16), (((1,), (1,)), ((), ())),
                           preferred_element_type=F32)


def _bdot_tn(a, b):
    return lax.dot_general(a.astype(BF16), b.astype(BF16), (((0,), (0,)), ((), ())),
                           preferred_element_type=F32)


def _split3(x):
    hi = x.astype(BF16)
    r1 = x - hi.astype(F32)
    mid = r1.astype(BF16)
    lo = (r1 - mid.astype(F32)).astype(BF16)
    return hi, mid, lo


def _exact_dot(m_bf16, x, pieces=3):
    d = lambda p: jnp.dot(m_bf16, p, preferred_element_type=F32)
    if pieces == 2:
        hi = x.astype(BF16)
        return d(hi) + d((x - hi.astype(F32)).astype(BF16))
    hi, mid, lo = _split3(x)
    return d(hi) + d(mid) + d(lo)


def _head_sum(x):
    r = lax.broadcasted_iota(jnp.int32, (MXU_TILE, MXU_TILE), 0)
    c = lax.broadcasted_iota(jnp.int32, (MXU_TILE, MXU_TILE), 1)
    ones_blk = ((r // RWKV_N) == (c // RWKV_N)).astype(BF16)
    xb = x.astype(BF16)
    out = [jnp.dot(xb[:, g * MXU_TILE:(g + 1) * MXU_TILE], ones_blk, preferred_element_type=F32)
           for g in range(x.shape[1] // MXU_TILE)]
    return jnp.concatenate(out, axis=1)


def _tril_incl(n):
    r = lax.broadcasted_iota(jnp.int32, (n, n), 0)
    c = lax.broadcasted_iota(jnp.int32, (n, n), 1)
    return (c <= r)


def _rmsnorm(x, g):
    return x * lax.rsqrt(jnp.mean(x * x, axis=-1, keepdims=True) + NORM_EPS) * g


def _interleave(streams, lag):
    active, waiting, rnd = [], list(streams), 0
    while active or waiting:
        if waiting and rnd % lag == 0:
            active.append(waiting.pop(0))
        for g in list(active):
            if next(g, _DONE) is _DONE:
                active.remove(g)
        rnd += 1


_DONE = object()


def _shift_rows(x, k, fill_rows):
    y = pltpu.roll(x, k, 0)
    row = lax.broadcasted_iota(jnp.int32, x.shape, 0)
    for i, f in enumerate(fill_rows):
        y = jnp.where(row == i, f, y)
    return y


def _inproj_kernel(x_ref, g_ref, wg_ref, wr_ref, og_ref, or_ref):
    h = _rmsnorm(x_ref[...], g_ref[...]).astype(BF16)
    og_ref[...] = jnp.dot(h, wg_ref[...], preferred_element_type=F32)
    or_ref[...] = jnp.dot(h, wr_ref[...], preferred_element_type=F32)


def _inproj(x2d, g, w_gla, w_rwkv, tm):
    t = x2d.shape[0]
    return pl.pallas_call(
        _inproj_kernel,
        grid=(t // tm,),
        in_specs=[
            pl.BlockSpec((tm, D_MODEL), lambda i: (i, 0)),
            pl.BlockSpec((1, D_MODEL), lambda i: (0, 0)),
            pl.BlockSpec((D_MODEL, GLA_SLAB), lambda i: (0, 0)),
            pl.BlockSpec((D_MODEL, RWKV_COLS), lambda i: (0, 0)),
        ],
        out_specs=[
            pl.BlockSpec((tm, GLA_SLAB), lambda i: (i, 0)),
            pl.BlockSpec((tm, RWKV_COLS), lambda i: (i, 0)),
        ],
        out_shape=[
            jax.ShapeDtypeStruct((t, GLA_SLAB), F32),
            jax.ShapeDtypeStruct((t, RWKV_COLS), F32),
        ],
        compiler_params=pltpu.CompilerParams(
            dimension_semantics=("arbitrary",), vmem_limit_bytes=VMEM_LIMIT_BYTES),
        name="inproj",
    )(x2d, g, w_gla, w_rwkv)


def _gla_kernel(p_ref, w2_ref, b_ref, nw_ref, s0_ref, o_ref, sout_ref, st_ref, *, n_chunks, nb, lag):
    ti = pl.program_id(1)

    @pl.when(ti == 0)
    def _():
        st_ref[...] = s0_ref[...]

    tl = n_chunks * CHUNK
    ri = lax.broadcasted_iota(jnp.int32, (tl, tl), 0)
    ci = lax.broadcasted_iota(jnp.int32, (tl, tl), 1)
    causal = ((ri // CHUNK) == (ci // CHUNK)) & (ci <= ri)
    causal_bf = causal.astype(BF16)
    lane = lax.broadcasted_iota(jnp.int32, (tl, LANES), 1)
    half = [lane < GLA_DK, lane >= GLA_DK]
    nw = nw_ref[...]
    heads = range(GLA_HEADS)
    chunks = range(n_chunks)
    crow = lambda x, c: x[c * CHUNK:(c + 1) * CHUNK]
    pair = lambda x, h: x[:, (h // 2) * LANES:(h // 2 + 1) * LANES]

    def stream(bi):
        logit = _bdot(p_ref[bi, :, _GLR:_GLR + LANES], w2_ref[...]) + b_ref[...]
        la = jax.nn.log_sigmoid(logit) / GLA_GATE_NORMALIZER
        yield
        cum = _exact_dot(causal_bf, la, pieces=2)
        tot = jnp.concatenate(
            [jnp.broadcast_to(cum[(c + 1) * CHUNK - 1:(c + 1) * CHUNK], (CHUNK, GLA_K))
             for c in chunks], axis=0)
        q = p_ref[bi, :, _GQ:_GQ + GLA_K] * (GLA_DK ** -0.5)
        k = p_ref[bi, :, _GK:_GK + GLA_K]
        yield
        q_e = q * jnp.exp(cum)
        k_e = k * jnp.exp(-cum)
        k_hat = k * jnp.exp(tot - cum)
        dec = jnp.exp(tot)
        v = [p_ref[bi, :, _GV + h * GLA_DV:_GV + (h + 1) * GLA_DV] for h in heads]
        yield

        sc = [_bdot_nt(jnp.where(half[h % 2], pair(q_e, h), 0.0), pair(k_e, h)) for h in heads]
        yield
        o_intra = [_bdot(jnp.where(causal, sc[h], 0.0), v[h]) for h in heads]
        yield

        kh_m = [jnp.where(half[h % 2], pair(k_hat, h), 0.0) for h in heads]
        upd = [[_bdot_tn(crow(v[h], c), crow(kh_m[h], c)) for c in chunks] for h in heads]
        yield
        st_in = []
        for h in heads:
            st = st_ref[bi, h]
            per_chunk = []
            for c in chunks:
                per_chunk.append(st)
                st = st * pair(dec, h)[c * CHUNK:c * CHUNK + 1] + upd[h][c]
            st_ref[bi, h] = st
            st_in.append(per_chunk)
        o_inter = [jnp.concatenate([_bdot_nt(crow(pair(q_e, h), c), st_in[h][c]) for c in chunks],
                                   axis=0) for h in heads]
        yield

        for h in heads:
            o = o_intra[h] + o_inter[h]
            gg = p_ref[bi, :, _GG + h * GLA_DV:_GG + (h + 1) * GLA_DV]
            o = o * lax.rsqrt(jnp.mean(o * o, axis=-1, keepdims=True) + NORM_EPS) * nw
            o = o * (gg * jax.nn.sigmoid(gg))
            o_ref[bi, :, h * GLA_DV:(h + 1) * GLA_DV] = o.astype(o_ref.dtype)
            if h % 2 == 1:
                yield
        sout_ref[bi] = st_ref[bi]

    _interleave([stream(bi) for bi in range(nb)], lag)


def _gla(p_gla, w2p, gate_b, norm_w, st0, tl):
    b, l, _ = p_gla.shape
    nb = GLA_SEQS_PER_STEP
    return pl.pallas_call(
        functools.partial(_gla_kernel, n_chunks=tl // CHUNK, nb=nb, lag=GLA_STREAM_LAG),
        grid=(b // nb, l // tl),
        in_specs=[
            pl.BlockSpec((nb, tl, GLA_SLAB), lambda i, j: (i, j, 0)),
            pl.BlockSpec((LANES, GLA_K), lambda i, j: (0, 0)),
            pl.BlockSpec((1, GLA_K), lambda i, j: (0, 0)),
            pl.BlockSpec((1, GLA_DV), lambda i, j: (0, 0)),
            pl.BlockSpec((nb, GLA_HEADS, GLA_DV, LANES), lambda i, j: (i, 0, 0, 0)),
        ],
        out_specs=[
            pl.BlockSpec((nb, tl, GLA_V), lambda i, j: (i, j, 0)),
            pl.BlockSpec((nb, GLA_HEADS, GLA_DV, LANES), lambda i, j: (i, 0, 0, 0)),
        ],
        out_shape=[
            jax.ShapeDtypeStruct((b, l, GLA_V), BF16),
            jax.ShapeDtypeStruct((b, GLA_HEADS, GLA_DV, LANES), F32),
        ],
        scratch_shapes=[pltpu.VMEM((nb, GLA_HEADS, GLA_DV, LANES), F32)],
        compiler_params=pltpu.CompilerParams(
            dimension_semantics=("arbitrary", "arbitrary"), vmem_limit_bytes=VMEM_LIMIT_BYTES),
        name="gla",
    )(p_gla, w2p, gate_b, norm_w, st0)


def _rwkv_kernel(p_ref, sh_ref, mu_ref, w0_ref, w2_ref, a0_ref, a2_ref, g2_ref, kk_ref, ka_ref,
                 rk_ref, lnw_ref, lnb_ref, s0_ref,
                 y_ref, sout_ref, shout_ref,
                 s_ref, prev_ref, *, n_chunks, tl, nb, lag):
    ti = pl.program_id(1)

    @pl.when(ti == 0)
    def _():
        s_ref[...] = s0_ref[...]
        prev_ref[...] = sh_ref[...]

    seg_sum = _head_sum
    tr = lax.broadcasted_iota(jnp.int32, (tl, tl), 0)
    tc = lax.broadcasted_iota(jnp.int32, (tl, tl), 1)
    chunk_tril = (((tr // CHUNK) == (tc // CHUNK)) & (tc <= tr)).astype(BF16)

    trow = lax.broadcasted_iota(jnp.int32, (CHUNK, LANES), 0)
    tcol = lax.broadcasted_iota(jnp.int32, (CHUNK, LANES), 1) & (RWKV_N - 1)
    strict = tcol < trow
    incl = tcol <= trow
    eye = (tcol == trow).astype(F32)
    ri = lax.broadcasted_iota(jnp.int32, (LANES, LANES), 0)
    ci = lax.broadcasted_iota(jnp.int32, (LANES, LANES), 1)
    same_head = (ri >> 6) == (ci >> 6)
    zeros_bd = jnp.zeros((LANES, LANES), BF16)

    def blockdiag(x):
        xb = x.astype(BF16)
        return jnp.where(same_head, jnp.concatenate([xb, xb], axis=0), zeros_bd)

    def stream(bi):
        p = p_ref[bi]
        prev = _shift_rows(p, 1, [prev_ref[bi]])
        prev_ref[bi] = p[tl - 1:tl, :]
        shout_ref[bi] = p[tl - 1:tl, :]
        xs = p + mu_ref[...] * (prev - p)
        xr = xs[:, _RR:_RR + RWKV_C]
        xk = xs[:, _RK:_RK + RWKV_C]
        xv = xs[:, _RV:_RV + RWKV_C]
        xwa = xs[:, _RWA:_RWA + LANES]
        xg = xs[:, _RG:_RG + RWKV_G_LR]
        yield
        w_raw = w0_ref[...] + _bdot(jnp.tanh(xwa), w2_ref[...])
        w_log = -(math.exp(-0.5)) * jax.nn.sigmoid(w_raw)
        a = jax.nn.sigmoid(a0_ref[...] + _bdot(xwa, a2_ref[...]))
        g = _bdot(jax.nn.sigmoid(xg), g2_ref[...])
        yield
        cl_tile = _exact_dot(chunk_tril, w_log, pieces=2)
        kk = xk * kk_ref[...]
        kk = kk * lax.rsqrt(jnp.maximum(seg_sum(kk * kk), 1e-24))
        yield
        kr = xk * (1.0 + (a - 1.0) * ka_ref[...])
        bonus = seg_sum(xr * kr * rk_ref[...]) * xv
        a_vec = -kk
        b_vec = kk * a
        yield

        units = [(slice(c * CHUNK, (c + 1) * CHUNK), slice(pr * LANES, (pr + 1) * LANES))
                 for c in range(n_chunks) for pr in range(RWKV_PAIRS)]
        n = range(len(units))
        a_t, r_t, b_t, k_t, b_h, k_h, v, dec = [], [], [], [], [], [], [], []
        for ui, (rows, cols) in enumerate(units):
            lw, cl = w_log[rows, cols], cl_tile[rows, cols]
            dec_row = jnp.exp(cl[CHUNK - 1:CHUNK, :])
            e_neg = jnp.exp(-cl)
            e_end = dec_row * e_neg
            av, bv, kv = a_vec[rows, cols], b_vec[rows, cols], kr[rows, cols]
            a_t.append(av * jnp.exp(cl - lw))
            r_t.append(xr[rows, cols] * jnp.exp(cl))
            b_t.append(bv * e_neg)
            k_t.append(kv * e_neg)
            b_h.append(bv * e_end)
            k_h.append(kv * e_end)
            v.append(xv[rows, cols])
            dec.append(dec_row)
            if ui % RWKV_PAIRS == RWKV_PAIRS - 1:
                yield
        gm = [_bdot_nt(jnp.concatenate([a_t[i], r_t[i]], axis=0),
                       jnp.concatenate([blockdiag(b_t[i]), blockdiag(k_t[i])], axis=0)) for i in n]
        yield
        a_ab = [jnp.where(strict, gm[i][:CHUNK, :LANES], 0.0) for i in n]
        a_ak = [jnp.where(strict, gm[i][:CHUNK, LANES:], 0.0) for i in n]
        r_b = [jnp.where(incl, gm[i][CHUNK:, :LANES], 0.0) for i in n]
        r_k = [jnp.where(incl, gm[i][CHUNK:, LANES:], 0.0) for i in n]
        inv = [eye + a_ab[i] for i in n]
        pw = [_bdot(a_ab[i], blockdiag(a_ab[i])) for i in n]
        yield
        for _ in range(4):
            both = [_bdot(jnp.concatenate([inv[i], pw[i]], axis=0), blockdiag(pw[i])) for i in n]
            inv = [inv[i] + both[i][:CHUNK] for i in n]
            pw = [both[i][CHUNK:] for i in n]
            yield
        inv = [inv[i] + _bdot(inv[i], blockdiag(pw[i])) for i in n]
        yield
        v_bd = [blockdiag(v[i]) for i in n]
        ark = [_bdot(jnp.concatenate([a_ak[i], r_k[i]], axis=0), v_bd[i]) for i in n]
        akv = [ark[i][:CHUNK] for i in n]
        rkv = [ark[i][CHUNK:] for i in n]
        yield
        w = [_bdot(inv[i], jnp.concatenate([blockdiag(a_t[i]), blockdiag(akv[i])], axis=1))
             for i in n]
        a_bar = [w[i][:, :LANES] for i in n]
        u0 = [w[i][:, LANES:] for i in n]
        yield
        ar = [jnp.concatenate([a_bar[i], r_t[i]], axis=0) for i in n]
        bk_h = [jnp.concatenate([b_h[i], k_h[i]], axis=0) for i in n]
        pairs = range(RWKV_PAIRS)
        y_chunks = []
        for c in range(n_chunks):
            ids = [c * RWKV_PAIRS + pr for pr in pairs]
            s = [s_ref[bi, pr] for pr in pairs]
            ars = [_bdot_nt(ar[i], s[pr]) for pr, i in zip(pairs, ids)]
            u = [ars[pr][:CHUNK] + u0[i] for pr, i in zip(pairs, ids)]
            yield
            upd = [_bdot_tn(jnp.concatenate([u[pr], v[i]], axis=0), bk_h[i])
                   for pr, i in zip(pairs, ids)]
            y_pairs = [ars[pr][CHUNK:] + rkv[i] + _bdot(r_b[i], blockdiag(u[pr]))
                       for pr, i in zip(pairs, ids)]
            for pr, i in zip(pairs, ids):
                s_ref[bi, pr] = s[pr] * dec[i] + jnp.where(same_head, upd[pr], 0.0)
            y_chunks.append(jnp.concatenate(y_pairs, axis=1))
            yield
        sout_ref[bi] = s_ref[bi]

        y = jnp.concatenate(y_chunks, axis=0)
        mean = seg_sum(y) * (1.0 / RWKV_N)
        yc = y - mean
        yield
        var = seg_sum(yc * yc) * (1.0 / RWKV_N)
        y = yc * lax.rsqrt(var + RWKV_GN_EPS) * lnw_ref[...] + lnb_ref[...]
        y_ref[bi] = ((y + bonus) * g).astype(y_ref.dtype)

    _interleave([stream(bi) for bi in range(nb)], lag)


def _rwkv(p_rwkv, shift_prev, prm, s0, tl):
    b, l, _ = p_rwkv.shape
    nb = RWKV_SEQS_PER_STEP
    row = lambda n: pl.BlockSpec((1, n), lambda i, j: (0, 0))
    mat = lambda m, n: pl.BlockSpec((m, n), lambda i, j: (0, 0))
    return pl.pallas_call(
        functools.partial(_rwkv_kernel, n_chunks=tl // CHUNK, tl=tl, nb=nb, lag=RWKV_STREAM_LAG),
        grid=(b // nb, l // tl),
        in_specs=[
            pl.BlockSpec((nb, tl, RWKV_COLS), lambda i, j: (i, j, 0)),
            pl.BlockSpec((nb, 1, RWKV_COLS), lambda i, j: (i, 0, 0)),
            row(RWKV_COLS),
            row(RWKV_C), mat(LANES, RWKV_C), row(RWKV_C), mat(LANES, RWKV_C), mat(RWKV_G_LR, RWKV_C),
            row(RWKV_C), row(RWKV_C), row(RWKV_C), row(RWKV_C), row(RWKV_C),
            pl.BlockSpec((nb, RWKV_PAIRS, LANES, LANES), lambda i, j: (i, 0, 0, 0)),
        ],
        out_specs=[
            pl.BlockSpec((nb, tl, RWKV_C), lambda i, j: (i, j, 0)),
            pl.BlockSpec((nb, RWKV_PAIRS, LANES, LANES), lambda i, j: (i, 0, 0, 0)),
            pl.BlockSpec((nb, 1, RWKV_COLS), lambda i, j: (i, 0, 0)),
        ],
        out_shape=[
            jax.ShapeDtypeStruct((b, l, RWKV_C), BF16),
            jax.ShapeDtypeStruct((b, RWKV_PAIRS, LANES, LANES), F32),
            jax.ShapeDtypeStruct((b, 1, RWKV_COLS), F32),
        ],
        scratch_shapes=[
            pltpu.VMEM((nb, RWKV_PAIRS, LANES, LANES), F32),
            pltpu.VMEM((nb, 1, RWKV_COLS), F32),
        ],
        compiler_params=pltpu.CompilerParams(
            dimension_semantics=("arbitrary", "arbitrary"), vmem_limit_bytes=VMEM_LIMIT_BYTES),
        name="rwkv",
    )(p_rwkv, shift_prev, prm["mu"], prm["w0"], prm["w2p"], prm["a0"], prm["a2p"], prm["g2"],
      prm["k_k"], prm["k_a"], prm["r_k"], prm["ln_w"], prm["ln_b"], s0)


def _post_kernel(x_ref, og_ref, yr_ref, wo_ref, nf_ref, wu_ref, wg_ref, cw_ref, cb_ref, wd_ref,
                 cprev_ref, nfin_ref, out_ref, cnew_ref, act_ref, carry_ref,
                 *, tiles_per_seq, tm, final_norm):
    i = pl.program_id(0)
    wo = wo_ref[...]
    mix = (jnp.dot(og_ref[...], wo[:GLA_V], preferred_element_type=F32)
           + jnp.dot(yr_ref[...], wo[GLA_V:], preferred_element_type=F32))
    x1 = x_ref[...] + mix
    out_ref[...] = x1
    h = _rmsnorm(x1, nf_ref[...]).astype(BF16)

    @pl.when(i % tiles_per_seq == 0)
    def _():
        carry_ref[...] = cprev_ref[0]

    for j in range(D_FF // MXU_TILE):
        cols = slice(j * MXU_TILE, (j + 1) * MXU_TILE)
        u = jnp.dot(h, wu_ref[:, cols], preferred_element_type=F32)
        gate = jnp.dot(h, wg_ref[:, cols], preferred_element_type=F32)
        p0, p1 = carry_ref[0:1, cols], carry_ref[1:2, cols]
        uc = (cb_ref[:, cols] + cw_ref[0:1, cols] * _shift_rows(u, 2, [p0, p1])
              + cw_ref[1:2, cols] * _shift_rows(u, 1, [p1]) + cw_ref[2:3, cols] * u)
        new_prev = u[tm - 2:tm, :]
        carry_ref[:, cols] = new_prev
        cnew_ref[0, :, cols] = new_prev
        act = 0.5 * uc * (1.0 + lax.erf(uc * (2.0 ** -0.5))) * gate
        act_ref[:, cols] = act.astype(BF16)

    x2 = out_ref[...] + jnp.dot(act_ref[...], wd_ref[...], preferred_element_type=F32)
    if final_norm:
        x2 = _rmsnorm(x2, nfin_ref[...])
    out_ref[...] = x2


def _post(x2d, o_gla, y_rwkv, prm, conv_prev, tm, seq_len, final_norm):
    t = x2d.shape[0]
    b = conv_prev.shape[0]
    tiles_per_seq = seq_len // tm
    const = lambda m, n: pl.BlockSpec((m, n), lambda i: (0, 0), pipeline_mode=pl.Buffered(1))
    return pl.pallas_call(
        functools.partial(_post_kernel, tiles_per_seq=tiles_per_seq, tm=tm, final_norm=final_norm),
        grid=(t // tm,),
        in_specs=[
            pl.BlockSpec((tm, D_MODEL), lambda i: (i, 0)),
            pl.BlockSpec((tm, GLA_V), lambda i: (i, 0)),
            pl.BlockSpec((tm, RWKV_C), lambda i: (i, 0)),
            const(D_MODEL, D_MODEL),
            const(1, D_MODEL),
            const(D_MODEL, D_FF),
            const(D_MODEL, D_FF),
            const(CONV_W, D_FF),
            const(1, D_FF),
            const(D_FF, D_MODEL),
            pl.BlockSpec((1, CONV_W - 1, D_FF), lambda i: (i // tiles_per_seq, 0, 0)),
            const(1, D_MODEL),
        ],
        out_specs=[
            pl.BlockSpec((tm, D_MODEL), lambda i: (i, 0)),
            pl.BlockSpec((1, CONV_W - 1, D_FF), lambda i: (i // tiles_per_seq, 0, 0)),
        ],
        out_shape=[
            jax.ShapeDtypeStruct((t, D_MODEL), F32),
            jax.ShapeDtypeStruct((b, CONV_W - 1, D_FF), F32),
        ],
        scratch_shapes=[
            pltpu.VMEM((tm, D_FF), BF16),
            pltpu.VMEM((CONV_W - 1, D_FF), F32),
        ],
        compiler_params=pltpu.CompilerParams(
            dimension_semantics=("arbitrary",), vmem_limit_bytes=VMEM_LIMIT_BYTES),
        name="post",
    )(x2d, o_gla, y_rwkv, prm["w_out"], prm["norm_ffn"], prm["w_up_u"], prm["w_up_g"],
      prm["conv_w"], prm["conv_b"], prm["w_down"], conv_prev, prm["norm_final"])


def _layer_params(l, P):
    w_in = P["w_in"][l]
    zpad = lambda a, rows_before, rows_total: jnp.zeros((rows_total, a.shape[1]), a.dtype).at[
        rows_before:rows_before + a.shape[0]].set(a)
    w_gla = jnp.concatenate(
        [w_in[:, :GLA_COLS], jnp.zeros((D_MODEL, GLA_SLAB - GLA_COLS), w_in.dtype)], axis=1)
    r = lambda a: a.reshape(1, -1)
    return dict(
        norm_mix=r(P["norm_mix"][l]),
        w_gla=w_gla.astype(BF16),
        w_rwkv=w_in[:, GLA_COLS:].astype(BF16),
        gate_w2p=zpad(P["gla_gate_w2"][l], 0, LANES).astype(BF16),
        gate_b=r(P["gla_gate_b"][l]),
        gla_norm_w=r(P["gla_norm_w"][l]),
        mu=r(P["rwkv_mu"][l]),
        w0=r(P["rwkv_w0"][l]),
        w2p=zpad(P["rwkv_w2"][l], 0, LANES).astype(BF16),
        a0=r(P["rwkv_a0"][l]),
        a2p=zpad(P["rwkv_a2"][l], RWKV_W_LR, LANES).astype(BF16),
        g2=P["rwkv_g2"][l].astype(BF16),
        k_k=r(P["rwkv_k_k"][l]), k_a=r(P["rwkv_k_a"][l]), r_k=r(P["rwkv_r_k"][l]),
        ln_w=r(P["rwkv_ln_w"][l]), ln_b=r(P["rwkv_ln_b"][l]),
        w_out=P["w_out"][l].astype(BF16),
        norm_ffn=r(P["norm_ffn"][l]),
        w_up_u=P["ffn_w_up"][l][:, :D_FF].astype(BF16),
        w_up_g=P["ffn_w_up"][l][:, D_FF:].astype(BF16),
        conv_w=P["ffn_conv_w"][l], conv_b=r(P["ffn_conv_b"][l]),
        w_down=P["ffn_w_down"][l].astype(BF16),
        norm_final=r(P["norm_final"]),
    )


def _gla_state_in(s):
    st = jnp.swapaxes(s, -1, -2)
    z = jnp.zeros_like(st)
    par = (jnp.arange(GLA_HEADS) % 2).reshape(1, GLA_HEADS, 1, 1)
    return jnp.where(par == 0, jnp.concatenate([st, z], -1), jnp.concatenate([z, st], -1))


def _gla_state_out(st):
    lo, hi = st[..., :GLA_DK], st[..., GLA_DK:]
    par = (jnp.arange(GLA_HEADS) % 2).reshape(1, GLA_HEADS, 1, 1)
    return jnp.swapaxes(jnp.where(par == 0, lo, hi), -1, -2)


def _rwkv_state_in(s):
    b = s.shape[0]
    s = s.reshape(b, RWKV_PAIRS, 2, RWKV_N, RWKV_N)
    z = jnp.zeros_like(s[:, :, 0])
    top = jnp.concatenate([s[:, :, 0], z], -1)
    bot = jnp.concatenate([z, s[:, :, 1]], -1)
    return jnp.concatenate([top, bot], -2)


def _rwkv_state_out(sb):
    b = sb.shape[0]
    s0 = sb[:, :, :RWKV_N, :RWKV_N]
    s1 = sb[:, :, RWKV_N:, RWKV_N:]
    return jnp.stack([s0, s1], axis=2).reshape(b, RWKV_HEADS, RWKV_N, RWKV_N)


def _tiles(seq_len):
    tm = min(512, seq_len)
    tl = min(256, seq_len)
    return tm, tl


def _trunk(x, s_gla, s_rwkv, s_shift, c_conv, layers):
    b, l, _ = x.shape
    tm, tl = _tiles(l)
    x2d = x.reshape(b * l, D_MODEL)
    new_gla, new_rwkv, new_shift, new_conv = [], [], [], []
    for li, prm in enumerate(layers):
        p_gla, p_rwkv = _inproj(x2d, prm["norm_mix"], prm["w_gla"], prm["w_rwkv"], tm)
        o_gla, st = _gla(p_gla.reshape(b, l, GLA_SLAB), prm["gate_w2p"], prm["gate_b"],
                         prm["gla_norm_w"], _gla_state_in(s_gla[li]), tl)
        y_rwkv, sb, sh = _rwkv(p_rwkv.reshape(b, l, RWKV_COLS), s_shift[li][:, None, :], prm,
                               _rwkv_state_in(s_rwkv[li]), tl)
        x2d, cc = _post(x2d, o_gla.reshape(b * l, GLA_V), y_rwkv.reshape(b * l, RWKV_C), prm,
                        c_conv[li], tm, l, li == len(layers) - 1)
        new_gla.append(_gla_state_out(st))
        new_rwkv.append(_rwkv_state_out(sb))
        new_shift.append(sh[:, 0, :])
        new_conv.append(cc)
    return (x2d.reshape(b, l, D_MODEL), jnp.stack(new_gla), jnp.stack(new_rwkv),
            jnp.stack(new_shift), jnp.stack(new_conv))


def kernel(x_prompt, x_sample, state_gla, state_rwkv, state_rwkv_shift, cache_ffn_conv, norm_mix, w_in, gla_gate_w2, gla_gate_b, gla_norm_w, rwkv_mu, rwkv_w0, rwkv_w2, rwkv_a0, rwkv_a2, rwkv_g2, rwkv_k_k, rwkv_k_a, rwkv_r_k, rwkv_ln_w, rwkv_ln_b, w_out, norm_ffn, ffn_w_up, ffn_conv_w, ffn_conv_b, ffn_w_down, norm_final):
    P = dict(norm_mix=norm_mix, w_in=w_in, gla_gate_w2=gla_gate_w2, gla_gate_b=gla_gate_b,
             gla_norm_w=gla_norm_w, rwkv_mu=rwkv_mu, rwkv_w0=rwkv_w0, rwkv_w2=rwkv_w2,
             rwkv_a0=rwkv_a0, rwkv_a2=rwkv_a2, rwkv_g2=rwkv_g2, rwkv_k_k=rwkv_k_k,
             rwkv_k_a=rwkv_k_a, rwkv_r_k=rwkv_r_k.reshape(DEPTH, RWKV_C), rwkv_ln_w=rwkv_ln_w,
             rwkv_ln_b=rwkv_ln_b, w_out=w_out, norm_ffn=norm_ffn, ffn_w_up=ffn_w_up,
             ffn_conv_w=ffn_conv_w, ffn_conv_b=ffn_conv_b, ffn_w_down=ffn_w_down,
             norm_final=norm_final)
    layers = [_layer_params(l, P) for l in range(DEPTH)]
    nb = x_prompt.shape[0]
    g0 = jnp.zeros((DEPTH, nb) + state_gla.shape[2:], state_gla.dtype)
    r0 = jnp.zeros((DEPTH, nb) + state_rwkv.shape[2:], state_rwkv.dtype)
    sh0 = jnp.zeros((DEPTH, nb) + state_rwkv_shift.shape[2:], state_rwkv_shift.dtype)
    c0 = jnp.zeros((DEPTH, nb) + cache_ffn_conv.shape[2:], cache_ffn_conv.dtype)
    y_p, gla_p, rwkv_p, shift_p, conv_p = _trunk(x_prompt, g0, r0, sh0, c0, layers)
    y_s, gla_s, rwkv_s, shift_s, conv_s = _trunk(
        x_sample, state_gla, state_rwkv, state_rwkv_shift, cache_ffn_conv, layers)
    return (y_p, y_s, gla_p, rwkv_p, shift_p, conv_p, gla_s, rwkv_s, shift_s, conv_s)
```

```python
import functools
import math

import jax
import jax.numpy as jnp
from jax import lax
from jax.experimental import pallas as pl
from jax.experimental.pallas import tpu as pltpu

F32 = jnp.float32
BF16 = jnp.bfloat16

D_MODEL = 1024
DEPTH = 2
CHUNK = 64
NORM_EPS = 1e-6
GLA_HEADS = 4
GLA_DV = 128
GLA_DK = 64
GLA_K = GLA_HEADS * GLA_DK
GLA_V = GLA_HEADS * GLA_DV
GLA_LR = 16
GLA_GATE_NORMALIZER = 16.0
RWKV_N = 64
RWKV_C = 512
RWKV_HEADS = 8
RWKV_PAIRS = RWKV_HEADS // 2
RWKV_W_LR = 64
RWKV_A_LR = 64
RWKV_G_LR = 128
RWKV_GN_EPS = 64e-5
GLA_COLS = 2 * GLA_K + 2 * GLA_V + GLA_LR
RWKV_COLS = 3 * RWKV_C + RWKV_W_LR + RWKV_A_LR + RWKV_G_LR
D_FF = 2816
CONV_W = 3

LANES = 128
MXU_TILE = 256
GLA_SLAB = 2 * GLA_K + 2 * GLA_V + LANES
VMEM_LIMIT_BYTES = 56 * 1024 * 1024
MIXER_SEQS_PER_STEP = 4
RWKV_STREAM_LAG = 8
GLA_STREAM_LAG = 3

_GQ, _GK, _GV, _GG, _GLR = 0, GLA_K, 2 * GLA_K, 2 * GLA_K + GLA_V, 2 * GLA_K + 2 * GLA_V
_RR, _RK, _RV, _RWA, _RG = 0, RWKV_C, 2 * RWKV_C, 3 * RWKV_C, 3 * RWKV_C + RWKV_W_LR + RWKV_A_LR


def _bdot(a, b):
    return jnp.dot(a.astype(BF16), b.astype(BF16), preferred_element_type=F32)


def _bdot_nt(a, b):
    return lax.dot_general(a.astype(BF16), b.astype(BF16), (((1,), (1,)), ((), ())),
                           preferred_element_type=F32)


def _bdot_tn(a, b):
    return lax.dot_general(a.astype(BF16), b.astype(BF16), (((0,), (0,)), ((), ())),
                           preferred_element_type=F32)


def _split3(x):
    hi = x.astype(BF16)
    r1 = x - hi.astype(F32)
    mid = r1.astype(BF16)
    lo = (r1 - mid.astype(F32)).astype(BF16)
    return hi, mid, lo


def _exact_dot(m_bf16, x, pieces=3):
    d = lambda p: jnp.dot(m_bf16, p, preferred_element_type=F32)
    if pieces == 2:
        hi = x.astype(BF16)
        return d(hi) + d((x - hi.astype(F32)).astype(BF16))
    hi, mid, lo = _split3(x)
    return d(hi) + d(mid) + d(lo)


def _head_sum(x):
    r = lax.broadcasted_iota(jnp.int32, (MXU_TILE, MXU_TILE), 0)
    c = lax.broadcasted_iota(jnp.int32, (MXU_TILE, MXU_TILE), 1)
    ones_blk = ((r // RWKV_N) == (c // RWKV_N)).astype(BF16)
    xb = x.astype(BF16)
    out = [jnp.dot(xb[:, g * MXU_TILE:(g + 1) * MXU_TILE], ones_blk, preferred_element_type=F32)
           for g in range(x.shape[1] // MXU_TILE)]
    return jnp.concatenate(out, axis=1)


def _tril_incl(n):
    r = lax.broadcasted_iota(jnp.int32, (n, n), 0)
    c = lax.broadcasted_iota(jnp.int32, (n, n), 1)
    return (c <= r)


def _rmsnorm(x, g):
    return x * lax.rsqrt(jnp.mean(x * x, axis=-1, keepdims=True) + NORM_EPS) * g


def _interleave(streams, lag):
    active, waiting, rnd = [], list(streams), 0
    while active or waiting:
        if waiting and rnd % lag == 0:
            active.append(waiting.pop(0))
        for g in list(active):
            if next(g, _DONE) is _DONE:
                active.remove(g)
        rnd += 1


_DONE = object()


def _shift_rows(x, k, fill_rows):
    axis = x.ndim - 2
    y = pltpu.roll(x, k, axis)
    row = lax.broadcasted_iota(jnp.int32, x.shape, axis)
    for i, f in enumerate(fill_rows):
        y = jnp.where(row == i, f, y)
    return y


def _inproj_kernel(x_ref, g_ref, wg_ref, wr_ref, og_ref, or_ref):
    h = _rmsnorm(x_ref[...], g_ref[...]).astype(BF16)
    og_ref[...] = jnp.dot(h, wg_ref[...], preferred_element_type=F32)
    or_ref[...] = jnp.dot(h, wr_ref[...], preferred_element_type=F32)


def _inproj(x2d, g, w_gla, w_rwkv, tm):
    t = x2d.shape[0]
    const = lambda m, n: pl.BlockSpec((m, n), lambda i: (0, 0), pipeline_mode=pl.Buffered(1))
    return pl.pallas_call(
        _inproj_kernel,
        grid=(t // tm,),
        in_specs=[
            pl.BlockSpec((tm, D_MODEL), lambda i: (i, 0)),
            const(1, D_MODEL),
            const(D_MODEL, GLA_SLAB),
            const(D_MODEL, RWKV_COLS),
        ],
        out_specs=[
            pl.BlockSpec((tm, GLA_SLAB), lambda i: (i, 0)),
            pl.BlockSpec((tm, RWKV_COLS), lambda i: (i, 0)),
        ],
        out_shape=[
            jax.ShapeDtypeStruct((t, GLA_SLAB), F32),
            jax.ShapeDtypeStruct((t, RWKV_COLS), F32),
        ],
        compiler_params=pltpu.CompilerParams(
            dimension_semantics=("arbitrary",), vmem_limit_bytes=VMEM_LIMIT_BYTES),
        name="inproj",
    )(x2d, g, w_gla, w_rwkv)


def _gla_streams(p_ref, w2_ref, b_ref, nw_ref, s0_ref, o_ref, sout_ref, st_ref, *, n_chunks, nb):
    ti = pl.program_id(1)

    @pl.when(ti == 0)
    def _():
        st_ref[...] = s0_ref[...]

    tl = n_chunks * CHUNK
    ri = lax.broadcasted_iota(jnp.int32, (tl, tl), 0)
    ci = lax.broadcasted_iota(jnp.int32, (tl, tl), 1)
    causal = ((ri // CHUNK) == (ci // CHUNK)) & (ci <= ri)
    causal_bf = causal.astype(BF16)
    lane = lax.broadcasted_iota(jnp.int32, (tl, LANES), 1)
    half = [lane < GLA_DK, lane >= GLA_DK]
    nw = nw_ref[...]
    heads = range(GLA_HEADS)
    chunks = range(n_chunks)
    crow = lambda x, c: x[c * CHUNK:(c + 1) * CHUNK]
    pair = lambda x, h: x[:, (h // 2) * LANES:(h // 2 + 1) * LANES]

    def stream(bi):
        logit = _bdot(p_ref[bi, :, _GLR:_GLR + LANES], w2_ref[...]) + b_ref[...]
        la = jax.nn.log_sigmoid(logit) / GLA_GATE_NORMALIZER
        yield
        cum = _exact_dot(causal_bf, la, pieces=2)
        tot = jnp.concatenate(
            [jnp.broadcast_to(cum[(c + 1) * CHUNK - 1:(c + 1) * CHUNK], (CHUNK, GLA_K))
             for c in chunks], axis=0)
        q = p_ref[bi, :, _GQ:_GQ + GLA_K] * (GLA_DK ** -0.5)
        k = p_ref[bi, :, _GK:_GK + GLA_K]
        yield
        q_e = q * jnp.exp(cum)
        k_e = k * jnp.exp(-cum)
        k_hat = k * jnp.exp(tot - cum)
        dec = jnp.exp(tot)
        v = [p_ref[bi, :, _GV + h * GLA_DV:_GV + (h + 1) * GLA_DV] for h in heads]
        yield

        sc = [_bdot_nt(jnp.where(half[h % 2], pair(q_e, h), 0.0), pair(k_e, h)) for h in heads]
        yield
        o_intra = [_bdot(jnp.where(causal, sc[h], 0.0), v[h]) for h in heads]
        yield

        kh_m = [jnp.where(half[h % 2], pair(k_hat, h), 0.0) for h in heads]
        upd = [[_bdot_tn(crow(v[h], c), crow(kh_m[h], c)) for c in chunks] for h in heads]
        yield
        st_in = []
        for h in heads:
            st = st_ref[bi, h]
            per_chunk = []
            for c in chunks:
                per_chunk.append(st)
                st = st * pair(dec, h)[c * CHUNK:c * CHUNK + 1] + upd[h][c]
            st_ref[bi, h] = st
            st_in.append(per_chunk)
        o_inter = [jnp.concatenate([_bdot_nt(crow(pair(q_e, h), c), st_in[h][c]) for c in chunks],
                                   axis=0) for h in heads]
        yield

        for h in heads:
            o = o_intra[h] + o_inter[h]
            gg = p_ref[bi, :, _GG + h * GLA_DV:_GG + (h + 1) * GLA_DV]
            o = o * lax.rsqrt(jnp.mean(o * o, axis=-1, keepdims=True) + NORM_EPS) * nw
            o = o * (gg * jax.nn.sigmoid(gg))
            o_ref[bi, :, h * GLA_DV:(h + 1) * GLA_DV] = o.astype(o_ref.dtype)
            if h % 2 == 1:
                yield
        sout_ref[bi] = st_ref[bi]

    return [stream(bi) for bi in range(nb)]


def _rwkv_streams(p_ref, sh_ref, mu_ref, w0_ref, w2_ref, a0_ref, a2_ref, g2_ref, kk_ref, ka_ref,
                  rk_ref, lnw_ref, lnb_ref, s0_ref,
                  y_ref, sout_ref, shout_ref,
                  s_ref, prev_ref, *, n_chunks, tl, nb):
    ti = pl.program_id(1)

    @pl.when(ti == 0)
    def _():
        s_ref[...] = s0_ref[...]
        prev_ref[...] = sh_ref[...]

    seg_sum = _head_sum
    tr = lax.broadcasted_iota(jnp.int32, (tl, tl), 0)
    tc = lax.broadcasted_iota(jnp.int32, (tl, tl), 1)
    chunk_tril = (((tr // CHUNK) == (tc // CHUNK)) & (tc <= tr)).astype(BF16)

    trow = lax.broadcasted_iota(jnp.int32, (CHUNK, LANES), 0)
    tcol = lax.broadcasted_iota(jnp.int32, (CHUNK, LANES), 1) & (RWKV_N - 1)
    strict = tcol < trow
    incl = tcol <= trow
    eye = (tcol == trow).astype(F32)
    ri = lax.broadcasted_iota(jnp.int32, (LANES, LANES), 0)
    ci = lax.broadcasted_iota(jnp.int32, (LANES, LANES), 1)
    same_head = (ri >> 6) == (ci >> 6)
    zeros_bd = jnp.zeros((LANES, LANES), BF16)

    def blockdiag(x):
        xb = x.astype(BF16)
        return jnp.where(same_head, jnp.concatenate([xb, xb], axis=0), zeros_bd)

    def stream(bi):
        p = p_ref[bi]
        prev = _shift_rows(p, 1, [prev_ref[bi]])
        prev_ref[bi] = p[tl - 1:tl, :]
        shout_ref[bi] = p[tl - 1:tl, :]
        xs = p + mu_ref[...] * (prev - p)
        xr = xs[:, _RR:_RR + RWKV_C]
        xk = xs[:, _RK:_RK + RWKV_C]
        xv = xs[:, _RV:_RV + RWKV_C]
        xwa = xs[:, _RWA:_RWA + LANES]
        xg = xs[:, _RG:_RG + RWKV_G_LR]
        yield
        w_raw = w0_ref[...] + _bdot(jnp.tanh(xwa), w2_ref[...])
        w_log = -(math.exp(-0.5)) * jax.nn.sigmoid(w_raw)
        a = jax.nn.sigmoid(a0_ref[...] + _bdot(xwa, a2_ref[...]))
        g = _bdot(jax.nn.sigmoid(xg), g2_ref[...])
        yield
        cl_tile = _exact_dot(chunk_tril, w_log, pieces=2)
        kk = xk * kk_ref[...]
        kk = kk * lax.rsqrt(jnp.maximum(seg_sum(kk * kk), 1e-24))
        yield
        kr = xk * (1.0 + (a - 1.0) * ka_ref[...])
        bonus = seg_sum(xr * kr * rk_ref[...]) * xv
        a_vec = -kk
        b_vec = kk * a
        yield

        units = [(slice(c * CHUNK, (c + 1) * CHUNK), slice(pr * LANES, (pr + 1) * LANES))
                 for c in range(n_chunks) for pr in range(RWKV_PAIRS)]
        n = range(len(units))
        a_t, r_t, b_t, k_t, b_h, k_h, v, dec = [], [], [], [], [], [], [], []
        for ui, (rows, cols) in enumerate(units):
            lw, cl = w_log[rows, cols], cl_tile[rows, cols]
            dec_row = jnp.exp(cl[CHUNK - 1:CHUNK, :])
            e_neg = jnp.exp(-cl)
            e_end = dec_row * e_neg
            av, bv, kv = a_vec[rows, cols], b_vec[rows, cols], kr[rows, cols]
            a_t.append(av * jnp.exp(cl - lw))
            r_t.append(xr[rows, cols] * jnp.exp(cl))
            b_t.append(bv * e_neg)
            k_t.append(kv * e_neg)
            b_h.append(bv * e_end)
            k_h.append(kv * e_end)
            v.append(xv[rows, cols])
            dec.append(dec_row)
            if ui % RWKV_PAIRS == RWKV_PAIRS - 1:
                yield
        gm = [_bdot_nt(jnp.concatenate([a_t[i], r_t[i]], axis=0),
                       jnp.concatenate([blockdiag(b_t[i]), blockdiag(k_t[i])], axis=0)) for i in n]
        yield
        a_ab = [jnp.where(strict, gm[i][:CHUNK, :LANES], 0.0) for i in n]
        a_ak = [jnp.where(strict, gm[i][:CHUNK, LANES:], 0.0) for i in n]
        r_b = [jnp.where(incl, gm[i][CHUNK:, :LANES], 0.0) for i in n]
        r_k = [jnp.where(incl, gm[i][CHUNK:, LANES:], 0.0) for i in n]
        inv = [eye + a_ab[i] for i in n]
        pw = [_bdot(a_ab[i], blockdiag(a_ab[i])) for i in n]
        yield
        for _ in range(4):
            both = [_bdot(jnp.concatenate([inv[i], pw[i]], axis=0), blockdiag(pw[i])) for i in n]
            inv = [inv[i] + both[i][:CHUNK] for i in n]
            pw = [both[i][CHUNK:] for i in n]
            yield
        inv = [inv[i] + _bdot(inv[i], blockdiag(pw[i])) for i in n]
        yield
        v_bd = [blockdiag(v[i]) for i in n]
        ark = [_bdot(jnp.concatenate([a_ak[i], r_k[i]], axis=0), v_bd[i]) for i in n]
        akv = [ark[i][:CHUNK] for i in n]
        rkv = [ark[i][CHUNK:] for i in n]
        yield
        w = [_bdot(inv[i], jnp.concatenate([blockdiag(a_t[i]), blockdiag(akv[i])], axis=1))
             for i in n]
        a_bar = [w[i][:, :LANES] for i in n]
        u0 = [w[i][:, LANES:] for i in n]
        yield
        ar = [jnp.concatenate([a_bar[i], r_t[i]], axis=0) for i in n]
        bk_h = [jnp.concatenate([b_h[i], k_h[i]], axis=0) for i in n]
        pairs = range(RWKV_PAIRS)
        y_chunks = []
        for c in range(n_chunks):
            ids = [c * RWKV_PAIRS + pr for pr in pairs]
            s = [s_ref[bi, pr] for pr in pairs]
            ars = [_bdot_nt(ar[i], s[pr]) for pr, i in zip(pairs, ids)]
            u = [ars[pr][:CHUNK] + u0[i] for pr, i in zip(pairs, ids)]
            yield
            upd = [_bdot_tn(jnp.concatenate([u[pr], v[i]], axis=0), bk_h[i])
                   for pr, i in zip(pairs, ids)]
            y_pairs = [ars[pr][CHUNK:] + rkv[i] + _bdot(r_b[i], blockdiag(u[pr]))
                       for pr, i in zip(pairs, ids)]
            for pr, i in zip(pairs, ids):
                s_ref[bi, pr] = s[pr] * dec[i] + jnp.where(same_head, upd[pr], 0.0)
            y_chunks.append(jnp.concatenate(y_pairs, axis=1))
            yield
        sout_ref[bi] = s_ref[bi]

        y = jnp.concatenate(y_chunks, axis=0)
        mean = seg_sum(y) * (1.0 / RWKV_N)
        yc = y - mean
        yield
        var = seg_sum(yc * yc) * (1.0 / RWKV_N)
        y = yc * lax.rsqrt(var + RWKV_GN_EPS) * lnw_ref[...] + lnb_ref[...]
        y_ref[bi] = ((y + bonus) * g).astype(y_ref.dtype)

    return [stream(bi) for bi in range(nb)]


def _gla_kernel(*refs, n_chunks, nb, lag):
    _interleave(_gla_streams(*refs, n_chunks=n_chunks, nb=nb), lag)


def _rwkv_kernel(*refs, n_chunks, tl, nb, lag):
    _interleave(_rwkv_streams(*refs, n_chunks=n_chunks, tl=tl, nb=nb), lag)


def _gla(p_gla, prm, st0, tl):
    b, l, _ = p_gla.shape
    nb = MIXER_SEQS_PER_STEP
    assert b % nb == 0 and l % tl == 0
    row = lambda n: pl.BlockSpec((1, n), lambda i, j: (0, 0))
    seq = lambda *dims: pl.BlockSpec((nb,) + dims, lambda i, j: (i,) + (0,) * len(dims))
    tile = lambda n: pl.BlockSpec((nb, tl, n), lambda i, j: (i, j, 0))
    return pl.pallas_call(
        functools.partial(_gla_kernel, n_chunks=tl // CHUNK, nb=nb, lag=GLA_STREAM_LAG),
        grid=(b // nb, l // tl),
        in_specs=[
            tile(GLA_SLAB), pl.BlockSpec((LANES, GLA_K), lambda i, j: (0, 0)), row(GLA_K),
            row(GLA_DV), seq(GLA_HEADS, GLA_DV, LANES),
        ],
        out_specs=[tile(GLA_V), seq(GLA_HEADS, GLA_DV, LANES)],
        out_shape=[
            jax.ShapeDtypeStruct((b, l, GLA_V), BF16),
            jax.ShapeDtypeStruct((b, GLA_HEADS, GLA_DV, LANES), F32),
        ],
        scratch_shapes=[pltpu.VMEM((nb, GLA_HEADS, GLA_DV, LANES), F32)],
        compiler_params=pltpu.CompilerParams(
            dimension_semantics=("arbitrary", "arbitrary"), vmem_limit_bytes=VMEM_LIMIT_BYTES),
        name="gla",
    )(p_gla, prm["gate_w2p"], prm["gate_b"], prm["gla_norm_w"], st0)


def _rwkv(p_rwkv, shift_prev, prm, s0, tl):
    b, l, _ = p_rwkv.shape
    nb = MIXER_SEQS_PER_STEP
    assert b % nb == 0 and l % tl == 0
    row = lambda n: pl.BlockSpec((1, n), lambda i, j: (0, 0))
    mat = lambda m, n: pl.BlockSpec((m, n), lambda i, j: (0, 0))
    seq = lambda *dims: pl.BlockSpec((nb,) + dims, lambda i, j: (i,) + (0,) * len(dims))
    tile = lambda n: pl.BlockSpec((nb, tl, n), lambda i, j: (i, j, 0))
    return pl.pallas_call(
        functools.partial(_rwkv_kernel, n_chunks=tl // CHUNK, tl=tl, nb=nb, lag=RWKV_STREAM_LAG),
        grid=(b // nb, l // tl),
        in_specs=[
            tile(RWKV_COLS), seq(1, RWKV_COLS), row(RWKV_COLS),
            row(RWKV_C), mat(LANES, RWKV_C), row(RWKV_C), mat(LANES, RWKV_C), mat(RWKV_G_LR, RWKV_C),
            row(RWKV_C), row(RWKV_C), row(RWKV_C), row(RWKV_C), row(RWKV_C),
            seq(RWKV_PAIRS, LANES, LANES),
        ],
        out_specs=[tile(RWKV_C), seq(RWKV_PAIRS, LANES, LANES), seq(1, RWKV_COLS)],
        out_shape=[
            jax.ShapeDtypeStruct((b, l, RWKV_C), BF16),
            jax.ShapeDtypeStruct((b, RWKV_PAIRS, LANES, LANES), F32),
            jax.ShapeDtypeStruct((b, 1, RWKV_COLS), F32),
        ],
        scratch_shapes=[
            pltpu.VMEM((nb, RWKV_PAIRS, LANES, LANES), F32),
            pltpu.VMEM((nb, 1, RWKV_COLS), F32),
        ],
        compiler_params=pltpu.CompilerParams(
            dimension_semantics=("arbitrary", "arbitrary"), vmem_limit_bytes=VMEM_LIMIT_BYTES),
        name="rwkv",
    )(p_rwkv, shift_prev, prm["mu"], prm["w0"], prm["w2p"], prm["a0"], prm["a2p"], prm["g2"],
      prm["k_k"], prm["k_a"], prm["r_k"], prm["ln_w"], prm["ln_b"], s0)


def _post_kernel(x_ref, og_ref, yr_ref, wo_ref, nf_ref, wu_ref, wg_ref, cw_ref, cb_ref, wd_ref,
                 cprev_ref, nfin_ref, out_ref, cnew_ref, act_ref, carry_ref,
                 *, tiles_per_seq, seqs_per_tile, tm, final_norm):
    i = pl.program_id(0)
    rows = tm // seqs_per_tile
    wo = wo_ref[...]
    mix = (jnp.dot(og_ref[...], wo[:GLA_V], preferred_element_type=F32)
           + jnp.dot(yr_ref[...], wo[GLA_V:], preferred_element_type=F32))
    x1 = x_ref[...] + mix
    out_ref[...] = x1
    h = _rmsnorm(x1, nf_ref[...]).astype(BF16)

    @pl.when(i % tiles_per_seq == 0)
    def _():
        carry_ref[...] = cprev_ref[...]

    for j in range(D_FF // MXU_TILE):
        cols = slice(j * MXU_TILE, (j + 1) * MXU_TILE)
        u = jnp.dot(h, wu_ref[:, cols], preferred_element_type=F32)
        gate = jnp.dot(h, wg_ref[:, cols], preferred_element_type=F32)
        u3 = u.reshape(seqs_per_tile, rows, MXU_TILE)
        p0, p1 = carry_ref[:, 0:1, cols], carry_ref[:, 1:2, cols]
        uc = (cb_ref[:, cols] + cw_ref[0:1, cols] * _shift_rows(u3, 2, [p0, p1])
              + cw_ref[1:2, cols] * _shift_rows(u3, 1, [p1]) + cw_ref[2:3, cols] * u3)
        uc = uc.reshape(tm, MXU_TILE)
        new_prev = u3[:, rows - 2:rows, :]
        carry_ref[:, :, cols] = new_prev
        cnew_ref[:, :, cols] = new_prev
        act = 0.5 * uc * (1.0 + lax.erf(uc * (2.0 ** -0.5))) * gate
        act_ref[:, cols] = act.astype(BF16)

    x2 = out_ref[...] + jnp.dot(act_ref[...], wd_ref[...], preferred_element_type=F32)
    if final_norm:
        x2 = _rmsnorm(x2, nfin_ref[...])
    out_ref[...] = x2


def _post(x2d, o_gla, y_rwkv, prm, conv_prev, tm, seq_len, final_norm):
    t = x2d.shape[0]
    b = conv_prev.shape[0]
    tiles_per_seq = max(seq_len // tm, 1)
    spt = max(tm // seq_len, 1)
    const = lambda m, n: pl.BlockSpec((m, n), lambda i: (0, 0), pipeline_mode=pl.Buffered(1))
    return pl.pallas_call(
        functools.partial(_post_kernel, tiles_per_seq=tiles_per_seq, seqs_per_tile=spt, tm=tm,
                          final_norm=final_norm),
        grid=(t // tm,),
        in_specs=[
            pl.BlockSpec((tm, D_MODEL), lambda i: (i, 0)),
            pl.BlockSpec((tm, GLA_V), lambda i: (i, 0)),
            pl.BlockSpec((tm, RWKV_C), lambda i: (i, 0)),
            const(D_MODEL, D_MODEL),
            const(1, D_MODEL),
            const(D_MODEL, D_FF),
            const(D_MODEL, D_FF),
            const(CONV_W, D_FF),
            const(1, D_FF),
            const(D_FF, D_MODEL),
            pl.BlockSpec((spt, CONV_W - 1, D_FF), lambda i: (i // tiles_per_seq, 0, 0)),
            const(1, D_MODEL),
        ],
        out_specs=[
            pl.BlockSpec((tm, D_MODEL), lambda i: (i, 0)),
            pl.BlockSpec((spt, CONV_W - 1, D_FF), lambda i: (i // tiles_per_seq, 0, 0)),
        ],
        out_shape=[
            jax.ShapeDtypeStruct((t, D_MODEL), F32),
            jax.ShapeDtypeStruct((b, CONV_W - 1, D_FF), F32),
        ],
        scratch_shapes=[
            pltpu.VMEM((tm, D_FF), BF16),
            pltpu.VMEM((spt, CONV_W - 1, D_FF), F32),
        ],
        compiler_params=pltpu.CompilerParams(
            dimension_semantics=("arbitrary",), vmem_limit_bytes=VMEM_LIMIT_BYTES),
        name="post",
    )(x2d, o_gla, y_rwkv, prm["w_out"], prm["norm_ffn"], prm["w_up_u"], prm["w_up_g"],
      prm["conv_w"], prm["conv_b"], prm["w_down"], conv_prev, prm["norm_final"])


def _layer_params(l, P):
    w_in = P["w_in"][l]
    zpad = lambda a, rows_before, rows_total: jnp.zeros((rows_total, a.shape[1]), a.dtype).at[
        rows_before:rows_before + a.shape[0]].set(a)
    w_gla = jnp.concatenate(
        [w_in[:, :GLA_COLS], jnp.zeros((D_MODEL, GLA_SLAB - GLA_COLS), w_in.dtype)], axis=1)
    r = lambda a: a.reshape(1, -1)
    return dict(
        norm_mix=r(P["norm_mix"][l]),
        w_gla=w_gla.astype(BF16),
        w_rwkv=w_in[:, GLA_COLS:].astype(BF16),
        gate_w2p=zpad(P["gla_gate_w2"][l], 0, LANES).astype(BF16),
        gate_b=r(P["gla_gate_b"][l]),
        gla_norm_w=r(P["gla_norm_w"][l]),
        mu=r(P["rwkv_mu"][l]),
        w0=r(P["rwkv_w0"][l]),
        w2p=zpad(P["rwkv_w2"][l], 0, LANES).astype(BF16),
        a0=r(P["rwkv_a0"][l]),
        a2p=zpad(P["rwkv_a2"][l], RWKV_W_LR, LANES).astype(BF16),
        g2=P["rwkv_g2"][l].astype(BF16),
        k_k=r(P["rwkv_k_k"][l]), k_a=r(P["rwkv_k_a"][l]), r_k=r(P["rwkv_r_k"][l]),
        ln_w=r(P["rwkv_ln_w"][l]), ln_b=r(P["rwkv_ln_b"][l]),
        w_out=P["w_out"][l].astype(BF16),
        norm_ffn=r(P["norm_ffn"][l]),
        w_up_u=P["ffn_w_up"][l][:, :D_FF].astype(BF16),
        w_up_g=P["ffn_w_up"][l][:, D_FF:].astype(BF16),
        conv_w=P["ffn_conv_w"][l], conv_b=r(P["ffn_conv_b"][l]),
        w_down=P["ffn_w_down"][l].astype(BF16),
        norm_final=r(P["norm_final"]),
    )


def _gla_state_in(s):
    st = jnp.swapaxes(s, -1, -2)
    z = jnp.zeros_like(st)
    par = (jnp.arange(GLA_HEADS) % 2).reshape(1, GLA_HEADS, 1, 1)
    return jnp.where(par == 0, jnp.concatenate([st, z], -1), jnp.concatenate([z, st], -1))


def _gla_state_out(st):
    lo, hi = st[..., :GLA_DK], st[..., GLA_DK:]
    par = (jnp.arange(GLA_HEADS) % 2).reshape(1, GLA_HEADS, 1, 1)
    return jnp.swapaxes(jnp.where(par == 0, lo, hi), -1, -2)


def _rwkv_state_in(s):
    b = s.shape[0]
    s = s.reshape(b, RWKV_PAIRS, 2, RWKV_N, RWKV_N)
    z = jnp.zeros_like(s[:, :, 0])
    top = jnp.concatenate([s[:, :, 0], z], -1)
    bot = jnp.concatenate([z, s[:, :, 1]], -1)
    return jnp.concatenate([top, bot], -2)


def _rwkv_state_out(sb):
    b = sb.shape[0]
    s0 = sb[:, :, :RWKV_N, :RWKV_N]
    s1 = sb[:, :, RWKV_N:, RWKV_N:]
    return jnp.stack([s0, s1], axis=2).reshape(b, RWKV_HEADS, RWKV_N, RWKV_N)


def _tiles(n_seqs, seq_len):
    tokens = n_seqs * seq_len
    tm = min(512, tokens)
    assert seq_len % tm == 0 or tm % seq_len == 0
    tm_in = min(2 * tm, tokens)
    assert tokens % tm_in == 0
    tl = min(256, seq_len)
    return tm_in, tm, tl


def _trunk(x, s_gla, s_rwkv, s_shift, c_conv, layers):
    b, l, _ = x.shape
    tm_in, tm, tl = _tiles(b, l)
    x2d = x.reshape(b * l, D_MODEL)
    new_gla, new_rwkv, new_shift, new_conv = [], [], [], []
    for li, prm in enumerate(layers):
        p_gla, p_rwkv = _inproj(x2d, prm["norm_mix"], prm["w_gla"], prm["w_rwkv"], tm_in)
        o_gla, st = _gla(p_gla.reshape(b, l, GLA_SLAB), prm, _gla_state_in(s_gla[li]), tl)
        y_rwkv, sb, sh = _rwkv(p_rwkv.reshape(b, l, RWKV_COLS), s_shift[li][:, None, :], prm,
                               _rwkv_state_in(s_rwkv[li]), tl)
        x2d, cc = _post(x2d, o_gla.reshape(b * l, GLA_V), y_rwkv.reshape(b * l, RWKV_C), prm,
                        c_conv[li], tm, l, li == len(layers) - 1)
        new_gla.append(_gla_state_out(st))
        new_rwkv.append(_rwkv_state_out(sb))
        new_shift.append(sh[:, 0, :])
        new_conv.append(cc)
    return (x2d.reshape(b, l, D_MODEL), jnp.stack(new_gla), jnp.stack(new_rwkv),
            jnp.stack(new_shift), jnp.stack(new_conv))


def kernel(x_prompt, x_sample, state_gla, state_rwkv, state_rwkv_shift, cache_ffn_conv, norm_mix, w_in, gla_gate_w2, gla_gate_b, gla_norm_w, rwkv_mu, rwkv_w0, rwkv_w2, rwkv_a0, rwkv_a2, rwkv_g2, rwkv_k_k, rwkv_k_a, rwkv_r_k, rwkv_ln_w, rwkv_ln_b, w_out, norm_ffn, ffn_w_up, ffn_conv_w, ffn_conv_b, ffn_w_down, norm_final):
    P = dict(norm_mix=norm_mix, w_in=w_in, gla_gate_w2=gla_gate_w2, gla_gate_b=gla_gate_b,
             gla_norm_w=gla_norm_w, rwkv_mu=rwkv_mu, rwkv_w0=rwkv_w0, rwkv_w2=rwkv_w2,
             rwkv_a0=rwkv_a0, rwkv_a2=rwkv_a2, rwkv_g2=rwkv_g2, rwkv_k_k=rwkv_k_k,
             rwkv_k_a=rwkv_k_a, rwkv_r_k=rwkv_r_k.reshape(DEPTH, RWKV_C), rwkv_ln_w=rwkv_ln_w,
             rwkv_ln_b=rwkv_ln_b, w_out=w_out, norm_ffn=norm_ffn, ffn_w_up=ffn_w_up,
             ffn_conv_w=ffn_conv_w, ffn_conv_b=ffn_conv_b, ffn_w_down=ffn_w_down,
             norm_final=norm_final)
    layers = [_layer_params(l, P) for l in range(DEPTH)]
    nb = x_prompt.shape[0]
    g0 = jnp.zeros((DEPTH, nb) + state_gla.shape[2:], state_gla.dtype)
    r0 = jnp.zeros((DEPTH, nb) + state_rwkv.shape[2:], state_rwkv.dtype)
    sh0 = jnp.zeros((DEPTH, nb) + state_rwkv_shift.shape[2:], state_rwkv_shift.dtype)
    c0 = jnp.zeros((DEPTH, nb) + cache_ffn_conv.shape[2:], cache_ffn_conv.dtype)
    y_p, gla_p, rwkv_p, shift_p, conv_p = _trunk(x_prompt, g0, r0, sh0, c0, layers)
    y_s, gla_s, rwkv_s, shift_s, conv_s = _trunk(
        x_sample, state_gla, state_rwkv, state_rwkv_shift, cache_ffn_conv, layers)
    return (y_p, y_s, gla_p, rwkv_p, shift_p, conv_p, gla_s, rwkv_s, shift_s, conv_s)
```

```python
import functools
import math

import jax
import jax.numpy as jnp
from jax import lax
from jax.experimental import pallas as pl
from jax.experimental.pallas import tpu as pltpu

F32 = jnp.float32
BF16 = jnp.bfloat16

D_MODEL = 1024
DEPTH = 2
CHUNK = 64
NORM_EPS = 1e-6
GLA_HEADS = 4
GLA_DV = 128
GLA_DK = 64
GLA_K = GLA_HEADS * GLA_DK
GLA_V = GLA_HEADS * GLA_DV
GLA_LR = 16
GLA_GATE_NORMALIZER = 16.0
RWKV_N = 64
RWKV_C = 512
RWKV_HEADS = 8
RWKV_PAIRS = RWKV_HEADS // 2
RWKV_W_LR = 64
RWKV_A_LR = 64
RWKV_G_LR = 128
RWKV_GN_EPS = 64e-5
GLA_COLS = 2 * GLA_K + 2 * GLA_V + GLA_LR
RWKV_COLS = 3 * RWKV_C + RWKV_W_LR + RWKV_A_LR + RWKV_G_LR
D_FF = 2816
CONV_W = 3

LANES = 128
MXU_TILE = 256
GLA_SLAB = 2 * GLA_K + 2 * GLA_V + LANES
VMEM_LIMIT_BYTES = 56 * 1024 * 1024
MIXER_SEQS_PER_STEP = 4
RWKV_STREAM_LAG = 8
GLA_STREAM_LAG = 3
POST_SUBTILES = 2
POST_STREAM_LAG = 2

_GQ, _GK, _GV, _GG, _GLR = 0, GLA_K, 2 * GLA_K, 2 * GLA_K + GLA_V, 2 * GLA_K + 2 * GLA_V
_RR, _RK, _RV, _RWA, _RG = 0, RWKV_C, 2 * RWKV_C, 3 * RWKV_C, 3 * RWKV_C + RWKV_W_LR + RWKV_A_LR


def _bdot(a, b):
    return jnp.dot(a.astype(BF16), b.astype(BF16), preferred_element_type=F32)


def _bdot_nt(a, b):
    return lax.dot_general(a.astype(BF16), b.astype(BF16), (((1,), (1,)), ((), ())),
                           preferred_element_type=F32)


def _bdot_tn(a, b):
    return lax.dot_general(a.astype(BF16), b.astype(BF16), (((0,), (0,)), ((), ())),
                           preferred_element_type=F32)


def _split3(x):
    hi = x.astype(BF16)
    r1 = x - hi.astype(F32)
    mid = r1.astype(BF16)
    lo = (r1 - mid.astype(F32)).astype(BF16)
    return hi, mid, lo


def _exact_dot(m_bf16, x, pieces=3):
    d = lambda p: jnp.dot(m_bf16, p, preferred_element_type=F32)
    if pieces == 2:
        hi = x.astype(BF16)
        return d(hi) + d((x - hi.astype(F32)).astype(BF16))
    hi, mid, lo = _split3(x)
    return d(hi) + d(mid) + d(lo)


def _head_sum(x):
    r = lax.broadcasted_iota(jnp.int32, (MXU_TILE, MXU_TILE), 0)
    c = lax.broadcasted_iota(jnp.int32, (MXU_TILE, MXU_TILE), 1)
    ones_blk = ((r // RWKV_N) == (c // RWKV_N)).astype(BF16)
    xb = x.astype(BF16)
    rows, groups = x.shape[0], x.shape[1] // MXU_TILE
    stacked = jnp.concatenate([xb[:, g * MXU_TILE:(g + 1) * MXU_TILE] for g in range(groups)], axis=0)
    out = jnp.dot(stacked, ones_blk, preferred_element_type=F32)
    return jnp.concatenate([out[g * rows:(g + 1) * rows] for g in range(groups)], axis=1)


def _tril_incl(n):
    r = lax.broadcasted_iota(jnp.int32, (n, n), 0)
    c = lax.broadcasted_iota(jnp.int32, (n, n), 1)
    return (c <= r)


def _rmsnorm(x, g):
    return x * lax.rsqrt(jnp.mean(x * x, axis=-1, keepdims=True) + NORM_EPS) * g


def _interleave(streams, lag):
    active, waiting, rnd = [], list(streams), 0
    while active or waiting:
        if waiting and rnd % lag == 0:
            active.append(waiting.pop(0))
        for g in list(active):
            if next(g, _DONE) is _DONE:
                active.remove(g)
        rnd += 1


_DONE = object()


def _shift_rows(x, k, fill_rows):
    axis = x.ndim - 2
    y = pltpu.roll(x, k, axis)
    row = lax.broadcasted_iota(jnp.int32, x.shape, axis)
    for i, f in enumerate(fill_rows):
        y = jnp.where(row == i, f, y)
    return y


def _inproj_kernel(x_ref, g_ref, wg_ref, wr_ref, og_ref, or_ref):
    h = _rmsnorm(x_ref[...], g_ref[...]).astype(BF16)
    og_ref[...] = jnp.dot(h, wg_ref[...], preferred_element_type=F32)
    or_ref[...] = jnp.dot(h, wr_ref[...], preferred_element_type=F32)


def _inproj(x2d, g, w_gla, w_rwkv, tm):
    t = x2d.shape[0]
    const = lambda m, n: pl.BlockSpec((m, n), lambda i: (0, 0), pipeline_mode=pl.Buffered(1))
    return pl.pallas_call(
        _inproj_kernel,
        grid=(t // tm,),
        in_specs=[
            pl.BlockSpec((tm, D_MODEL), lambda i: (i, 0)),
            const(1, D_MODEL),
            const(D_MODEL, GLA_SLAB),
            const(D_MODEL, RWKV_COLS),
        ],
        out_specs=[
            pl.BlockSpec((tm, GLA_SLAB), lambda i: (i, 0)),
            pl.BlockSpec((tm, RWKV_COLS), lambda i: (i, 0)),
        ],
        out_shape=[
            jax.ShapeDtypeStruct((t, GLA_SLAB), F32),
            jax.ShapeDtypeStruct((t, RWKV_COLS), F32),
        ],
        compiler_params=pltpu.CompilerParams(
            dimension_semantics=("arbitrary",), vmem_limit_bytes=VMEM_LIMIT_BYTES),
        name="inproj",
    )(x2d, g, w_gla, w_rwkv)


def _gla_streams(p_ref, w2_ref, b_ref, nw_ref, s0_ref, o_ref, sout_ref, st_ref, *, n_chunks, nb):
    ti = pl.program_id(1)

    @pl.when(ti == 0)
    def _():
        st_ref[...] = s0_ref[...]

    tl = n_chunks * CHUNK
    ri = lax.broadcasted_iota(jnp.int32, (tl, tl), 0)
    ci = lax.broadcasted_iota(jnp.int32, (tl, tl), 1)
    causal = ((ri // CHUNK) == (ci // CHUNK)) & (ci <= ri)
    causal_bf = causal.astype(BF16)
    lane = lax.broadcasted_iota(jnp.int32, (tl, LANES), 1)
    half = [lane < GLA_DK, lane >= GLA_DK]
    nw = nw_ref[...]
    heads = range(GLA_HEADS)
    chunks = range(n_chunks)
    crow = lambda x, c: x[c * CHUNK:(c + 1) * CHUNK]
    pair = lambda x, h: x[:, (h // 2) * LANES:(h // 2 + 1) * LANES]

    def stream(bi):
        logit = _bdot(p_ref[bi, :, _GLR:_GLR + LANES], w2_ref[...]) + b_ref[...]
        la = jax.nn.log_sigmoid(logit) / GLA_GATE_NORMALIZER
        yield
        cum = _exact_dot(causal_bf, la, pieces=2)
        tot = jnp.concatenate(
            [jnp.broadcast_to(cum[(c + 1) * CHUNK - 1:(c + 1) * CHUNK], (CHUNK, GLA_K))
             for c in chunks], axis=0)
        q = p_ref[bi, :, _GQ:_GQ + GLA_K] * (GLA_DK ** -0.5)
        k = p_ref[bi, :, _GK:_GK + GLA_K]
        yield
        q_e = q * jnp.exp(cum)
        k_e = k * jnp.exp(-cum)
        k_hat = k * jnp.exp(tot - cum)
        dec = jnp.exp(tot)
        v = [p_ref[bi, :, _GV + h * GLA_DV:_GV + (h + 1) * GLA_DV] for h in heads]
        yield

        sc = [_bdot_nt(jnp.where(half[h % 2], pair(q_e, h), 0.0), pair(k_e, h)) for h in heads]
        yield
        o_intra = [_bdot(jnp.where(causal, sc[h], 0.0), v[h]) for h in heads]
        yield

        kh_m = [jnp.where(half[h % 2], pair(k_hat, h), 0.0) for h in heads]
        upd = [[_bdot_tn(crow(v[h], c), crow(kh_m[h], c)) for c in chunks] for h in heads]
        yield
        st_in = []
        for h in heads:
            st = st_ref[bi, h]
            per_chunk = []
            for c in chunks:
                per_chunk.append(st)
                st = st * pair(dec, h)[c * CHUNK:c * CHUNK + 1] + upd[h][c]
            st_ref[bi, h] = st
            st_in.append(per_chunk)
        o_inter = [jnp.concatenate([_bdot_nt(crow(pair(q_e, h), c), st_in[h][c]) for c in chunks],
                                   axis=0) for h in heads]
        yield

        for h in heads:
            o = o_intra[h] + o_inter[h]
            gg = p_ref[bi, :, _GG + h * GLA_DV:_GG + (h + 1) * GLA_DV]
            o = o * lax.rsqrt(jnp.mean(o * o, axis=-1, keepdims=True) + NORM_EPS) * nw
            o = o * (gg * jax.nn.sigmoid(gg))
            o_ref[bi, :, h * GLA_DV:(h + 1) * GLA_DV] = o.astype(o_ref.dtype)
            if h % 2 == 1:
                yield
        sout_ref[bi] = st_ref[bi]

    return [stream(bi) for bi in range(nb)]


def _rwkv_streams(p_ref, sh_ref, mu_ref, w0_ref, w2_ref, a0_ref, a2_ref, g2_ref, kk_ref, ka_ref,
                  rk_ref, lnw_ref, lnb_ref, s0_ref,
                  y_ref, sout_ref, shout_ref,
                  s_ref, prev_ref, *, n_chunks, tl, nb):
    ti = pl.program_id(1)

    @pl.when(ti == 0)
    def _():
        s_ref[...] = s0_ref[...]
        prev_ref[...] = sh_ref[...]

    seg_sum = _head_sum
    tr = lax.broadcasted_iota(jnp.int32, (tl, tl), 0)
    tc = lax.broadcasted_iota(jnp.int32, (tl, tl), 1)
    chunk_tril = (((tr // CHUNK) == (tc // CHUNK)) & (tc <= tr)).astype(BF16)

    trow = lax.broadcasted_iota(jnp.int32, (CHUNK, LANES), 0)
    tcol = lax.broadcasted_iota(jnp.int32, (CHUNK, LANES), 1) & (RWKV_N - 1)
    strict = tcol < trow
    incl = tcol <= trow
    eye = (tcol == trow).astype(F32)
    ri = lax.broadcasted_iota(jnp.int32, (LANES, LANES), 0)
    ci = lax.broadcasted_iota(jnp.int32, (LANES, LANES), 1)
    same_head = (ri >> 6) == (ci >> 6)
    zeros_bd = jnp.zeros((LANES, LANES), BF16)

    def blockdiag(x):
        xb = x.astype(BF16)
        return jnp.where(same_head, jnp.concatenate([xb, xb], axis=0), zeros_bd)

    def stream(bi):
        p = p_ref[bi]
        prev = _shift_rows(p, 1, [prev_ref[bi]])
        prev_ref[bi] = p[tl - 1:tl, :]
        shout_ref[bi] = p[tl - 1:tl, :]
        xs = p + mu_ref[...] * (prev - p)
        xr = xs[:, _RR:_RR + RWKV_C]
        xk = xs[:, _RK:_RK + RWKV_C]
        xv = xs[:, _RV:_RV + RWKV_C]
        xwa = xs[:, _RWA:_RWA + LANES]
        xg = xs[:, _RG:_RG + RWKV_G_LR]
        yield
        w_raw = w0_ref[...] + _bdot(jnp.tanh(xwa), w2_ref[...])
        w_log = -(math.exp(-0.5)) * jax.nn.sigmoid(w_raw)
        a = jax.nn.sigmoid(a0_ref[...] + _bdot(xwa, a2_ref[...]))
        g = _bdot(jax.nn.sigmoid(xg), g2_ref[...])
        yield
        cl_tile = _exact_dot(chunk_tril, w_log, pieces=2)
        kk = xk * kk_ref[...]
        kk = kk * lax.rsqrt(jnp.maximum(seg_sum(kk * kk), 1e-24))
        yield
        kr = xk * (1.0 + (a - 1.0) * ka_ref[...])
        bonus = seg_sum(xr * kr * rk_ref[...]) * xv
        a_vec = -kk
        b_vec = kk * a
        yield

        units = [(slice(c * CHUNK, (c + 1) * CHUNK), slice(pr * LANES, (pr + 1) * LANES))
                 for c in range(n_chunks) for pr in range(RWKV_PAIRS)]
        n = range(len(units))
        a_t, r_t, b_t, k_t, b_h, k_h, v, dec = [], [], [], [], [], [], [], []
        for ui, (rows, cols) in enumerate(units):
            lw, cl = w_log[rows, cols], cl_tile[rows, cols]
            dec_row = jnp.exp(cl[CHUNK - 1:CHUNK, :])
            e_neg = jnp.exp(-cl)
            e_end = dec_row * e_neg
            av, bv, kv = a_vec[rows, cols], b_vec[rows, cols], kr[rows, cols]
            a_t.append(av * jnp.exp(cl - lw))
            r_t.append(xr[rows, cols] * jnp.exp(cl))
            b_t.append(bv * e_neg)
            k_t.append(kv * e_neg)
            b_h.append(bv * e_end)
            k_h.append(kv * e_end)
            v.append(xv[rows, cols])
            dec.append(dec_row)
            if ui % RWKV_PAIRS == RWKV_PAIRS - 1:
                yield
        gm = [_bdot_nt(jnp.concatenate([a_t[i], r_t[i]], axis=0),
                       jnp.concatenate([blockdiag(b_t[i]), blockdiag(k_t[i])], axis=0)) for i in n]
        yield
        a_ab = [jnp.where(strict, gm[i][:CHUNK, :LANES], 0.0) for i in n]
        a_ak = [jnp.where(strict, gm[i][:CHUNK, LANES:], 0.0) for i in n]
        r_b = [jnp.where(incl, gm[i][CHUNK:, :LANES], 0.0) for i in n]
        r_k = [jnp.where(incl, gm[i][CHUNK:, LANES:], 0.0) for i in n]
        inv = [eye + a_ab[i] for i in n]
        pw = [_bdot(a_ab[i], blockdiag(a_ab[i])) for i in n]
        yield
        for _ in range(4):
            both = [_bdot(jnp.concatenate([inv[i], pw[i]], axis=0), blockdiag(pw[i])) for i in n]
            inv = [inv[i] + both[i][:CHUNK] for i in n]
            pw = [both[i][CHUNK:] for i in n]
            yield
        inv = [inv[i] + _bdot(inv[i], blockdiag(pw[i])) for i in n]
        yield
        v_bd = [blockdiag(v[i]) for i in n]
        ark = [_bdot(jnp.concatenate([a_ak[i], r_k[i]], axis=0), v_bd[i]) for i in n]
        akv = [ark[i][:CHUNK] for i in n]
        rkv = [ark[i][CHUNK:] for i in n]
        yield
        w = [_bdot(inv[i], jnp.concatenate([blockdiag(a_t[i]), blockdiag(akv[i])], axis=1))
             for i in n]
        a_bar = [w[i][:, :LANES] for i in n]
        u0 = [w[i][:, LANES:] for i in n]
        yield
        ar = [jnp.concatenate([a_bar[i], r_t[i]], axis=0) for i in n]
        bk_h = [jnp.concatenate([b_h[i], k_h[i]], axis=0) for i in n]
        pairs = range(RWKV_PAIRS)
        y_chunks = []
        for c in range(n_chunks):
            ids = [c * RWKV_PAIRS + pr for pr in pairs]
            s = [s_ref[bi, pr] for pr in pairs]
            ars = [_bdot_nt(ar[i], s[pr]) for pr, i in zip(pairs, ids)]
            u = [ars[pr][:CHUNK] + u0[i] for pr, i in zip(pairs, ids)]
            yield
            upd = [_bdot_tn(jnp.concatenate([u[pr], v[i]], axis=0), bk_h[i])
                   for pr, i in zip(pairs, ids)]
            y_pairs = [ars[pr][CHUNK:] + rkv[i] + _bdot(r_b[i], blockdiag(u[pr]))
                       for pr, i in zip(pairs, ids)]
            for pr, i in zip(pairs, ids):
                s_ref[bi, pr] = s[pr] * dec[i] + jnp.where(same_head, upd[pr], 0.0)
            y_chunks.append(jnp.concatenate(y_pairs, axis=1))
            yield
        sout_ref[bi] = s_ref[bi]

        y = jnp.concatenate(y_chunks, axis=0)
        mean = seg_sum(y) * (1.0 / RWKV_N)
        yc = y - mean
        yield
        var = seg_sum(yc * yc) * (1.0 / RWKV_N)
        y = yc * lax.rsqrt(var + RWKV_GN_EPS) * lnw_ref[...] + lnb_ref[...]
        y_ref[bi] = ((y + bonus) * g).astype(y_ref.dtype)

    return [stream(bi) for bi in range(nb)]


def _gla_kernel(*refs, n_chunks, nb, lag):
    _interleave(_gla_streams(*refs, n_chunks=n_chunks, nb=nb), lag)


def _rwkv_kernel(*refs, n_chunks, tl, nb, lag):
    _interleave(_rwkv_streams(*refs, n_chunks=n_chunks, tl=tl, nb=nb), lag)


def _gla(p_gla, prm, st0, tl):
    b, l, _ = p_gla.shape
    nb = MIXER_SEQS_PER_STEP
    assert b % nb == 0 and l % tl == 0
    row = lambda n: pl.BlockSpec((1, n), lambda i, j: (0, 0))
    seq = lambda *dims: pl.BlockSpec((nb,) + dims, lambda i, j: (i,) + (0,) * len(dims))
    tile = lambda n: pl.BlockSpec((nb, tl, n), lambda i, j: (i, j, 0))
    return pl.pallas_call(
        functools.partial(_gla_kernel, n_chunks=tl // CHUNK, nb=nb, lag=GLA_STREAM_LAG),
        grid=(b // nb, l // tl),
        in_specs=[
            tile(GLA_SLAB), pl.BlockSpec((LANES, GLA_K), lambda i, j: (0, 0)), row(GLA_K),
            row(GLA_DV), seq(GLA_HEADS, GLA_DV, LANES),
        ],
        out_specs=[tile(GLA_V), seq(GLA_HEADS, GLA_DV, LANES)],
        out_shape=[
            jax.ShapeDtypeStruct((b, l, GLA_V), BF16),
            jax.ShapeDtypeStruct((b, GLA_HEADS, GLA_DV, LANES), F32),
        ],
        scratch_shapes=[pltpu.VMEM((nb, GLA_HEADS, GLA_DV, LANES), F32)],
        compiler_params=pltpu.CompilerParams(
            dimension_semantics=("arbitrary", "arbitrary"), vmem_limit_bytes=VMEM_LIMIT_BYTES),
        name="gla",
    )(p_gla, prm["gate_w2p"], prm["gate_b"], prm["gla_norm_w"], st0)


def _rwkv(p_rwkv, shift_prev, prm, s0, tl):
    b, l, _ = p_rwkv.shape
    nb = MIXER_SEQS_PER_STEP
    assert b % nb == 0 and l % tl == 0
    row = lambda n: pl.BlockSpec((1, n), lambda i, j: (0, 0))
    mat = lambda m, n: pl.BlockSpec((m, n), lambda i, j: (0, 0))
    seq = lambda *dims: pl.BlockSpec((nb,) + dims, lambda i, j: (i,) + (0,) * len(dims))
    tile = lambda n: pl.BlockSpec((nb, tl, n), lambda i, j: (i, j, 0))
    return pl.pallas_call(
        functools.partial(_rwkv_kernel, n_chunks=tl // CHUNK, tl=tl, nb=nb, lag=RWKV_STREAM_LAG),
        grid=(b // nb, l // tl),
        in_specs=[
            tile(RWKV_COLS), seq(1, RWKV_COLS), row(RWKV_COLS),
            row(RWKV_C), mat(LANES, RWKV_C), row(RWKV_C), mat(LANES, RWKV_C), mat(RWKV_G_LR, RWKV_C),
            row(RWKV_C), row(RWKV_C), row(RWKV_C), row(RWKV_C), row(RWKV_C),
            seq(RWKV_PAIRS, LANES, LANES),
        ],
        out_specs=[tile(RWKV_C), seq(RWKV_PAIRS, LANES, LANES), seq(1, RWKV_COLS)],
        out_shape=[
            jax.ShapeDtypeStruct((b, l, RWKV_C), BF16),
            jax.ShapeDtypeStruct((b, RWKV_PAIRS, LANES, LANES), F32),
            jax.ShapeDtypeStruct((b, 1, RWKV_COLS), F32),
        ],
        scratch_shapes=[
            pltpu.VMEM((nb, RWKV_PAIRS, LANES, LANES), F32),
            pltpu.VMEM((nb, 1, RWKV_COLS), F32),
        ],
        compiler_params=pltpu.CompilerParams(
            dimension_semantics=("arbitrary", "arbitrary"), vmem_limit_bytes=VMEM_LIMIT_BYTES),
        name="rwkv",
    )(p_rwkv, shift_prev, prm["mu"], prm["w0"], prm["w2p"], prm["a0"], prm["a2p"], prm["g2"],
      prm["k_k"], prm["k_a"], prm["r_k"], prm["ln_w"], prm["ln_b"], s0)


def _post_kernel(x_ref, og_ref, yr_ref, wo_ref, nf_ref, wup_ref, cw_ref, cb_ref, wd_ref,
                 cprev_ref, nfin_ref, out_ref, cnew_ref, act_ref, carry_ref,
                 *, tiles_per_seq, seqs_per_tile, tm, final_norm):
    i = pl.program_id(0)

    @pl.when(i % tiles_per_seq == 0)
    def _():
        carry_ref[...] = cprev_ref[...]

    ts = tm // POST_SUBTILES
    seqs_sub = max(seqs_per_tile // POST_SUBTILES, 1)
    rows = ts // seqs_sub

    def stream(sub):
        r = slice(sub * ts, (sub + 1) * ts)
        sq = slice(sub * seqs_sub, (sub + 1) * seqs_sub) if seqs_per_tile > 1 else slice(0, 1)
        wo = wo_ref[...]
        mix = (jnp.dot(og_ref[r, :], wo[:GLA_V], preferred_element_type=F32)
               + jnp.dot(yr_ref[r, :], wo[GLA_V:], preferred_element_type=F32))
        x1 = x_ref[r, :] + mix
        out_ref[r, :] = x1
        h = _rmsnorm(x1, nf_ref[...]).astype(BF16)
        yield
        for j in range(D_FF // MXU_TILE):
            cols = slice(j * MXU_TILE, (j + 1) * MXU_TILE)
            gcols = slice(D_FF + j * MXU_TILE, D_FF + (j + 1) * MXU_TILE)
            u = jnp.dot(h, wup_ref[:, cols], preferred_element_type=F32)
            gate = jnp.dot(h, wup_ref[:, gcols], preferred_element_type=F32)
            u3 = u.reshape(seqs_sub, rows, MXU_TILE)
            p0, p1 = carry_ref[sq, 0:1, cols], carry_ref[sq, 1:2, cols]
            uc = (cb_ref[:, cols] + cw_ref[0:1, cols] * _shift_rows(u3, 2, [p0, p1])
                  + cw_ref[1:2, cols] * _shift_rows(u3, 1, [p1]) + cw_ref[2:3, cols] * u3)
            uc = uc.reshape(ts, MXU_TILE)
            new_prev = u3[:, rows - 2:rows, :]
            carry_ref[sq, :, cols] = new_prev
            cnew_ref[sq, :, cols] = new_prev
            act = 0.5 * uc * (1.0 + lax.erf(uc * (2.0 ** -0.5))) * gate
            act_ref[r, cols] = act.astype(BF16)
            yield
        x2 = out_ref[r, :] + jnp.dot(act_ref[r, :], wd_ref[...], preferred_element_type=F32)
        if final_norm:
            x2 = _rmsnorm(x2, nfin_ref[...])
        out_ref[r, :] = x2

    _interleave([stream(sub) for sub in range(POST_SUBTILES)], POST_STREAM_LAG)


def _post(x2d, o_gla, y_rwkv, prm, conv_prev, tm, seq_len, final_norm):
    t = x2d.shape[0]
    b = conv_prev.shape[0]
    tiles_per_seq = max(seq_len // tm, 1)
    spt = max(tm // seq_len, 1)
    const = lambda m, n: pl.BlockSpec((m, n), lambda i: (0, 0), pipeline_mode=pl.Buffered(1))
    return pl.pallas_call(
        functools.partial(_post_kernel, tiles_per_seq=tiles_per_seq, seqs_per_tile=spt, tm=tm,
                          final_norm=final_norm),
        grid=(t // tm,),
        in_specs=[
            pl.BlockSpec((tm, D_MODEL), lambda i: (i, 0)),
            pl.BlockSpec((tm, GLA_V), lambda i: (i, 0)),
            pl.BlockSpec((tm, RWKV_C), lambda i: (i, 0)),
            const(D_MODEL, D_MODEL),
            const(1, D_MODEL),
            const(D_MODEL, 2 * D_FF),
            const(CONV_W, D_FF),
            const(1, D_FF),
            const(D_FF, D_MODEL),
            pl.BlockSpec((spt, CONV_W - 1, D_FF), lambda i: (i // tiles_per_seq, 0, 0)),
            const(1, D_MODEL),
        ],
        out_specs=[
            pl.BlockSpec((tm, D_MODEL), lambda i: (i, 0)),
            pl.BlockSpec((spt, CONV_W - 1, D_FF), lambda i: (i // tiles_per_seq, 0, 0)),
        ],
        out_shape=[
            jax.ShapeDtypeStruct((t, D_MODEL), F32),
            jax.ShapeDtypeStruct((b, CONV_W - 1, D_FF), F32),
        ],
        scratch_shapes=[
            pltpu.VMEM((tm, D_FF), BF16),
            pltpu.VMEM((spt, CONV_W - 1, D_FF), F32),
        ],
        compiler_params=pltpu.CompilerParams(
            dimension_semantics=("arbitrary",), vmem_limit_bytes=VMEM_LIMIT_BYTES),
        name="post",
    )(x2d, o_gla, y_rwkv, prm["w_out"], prm["norm_ffn"], prm["w_up"],
      prm["conv_w"], prm["conv_b"], prm["w_down"], conv_prev, prm["norm_final"])


def _layer_params(l, P):
    w_in = P["w_in"][l]
    zpad = lambda a, rows_before, rows_total: jnp.zeros((rows_total, a.shape[1]), a.dtype).at[
        rows_before:rows_before + a.shape[0]].set(a)
    w_gla = jnp.concatenate(
        [w_in[:, :GLA_COLS], jnp.zeros((D_MODEL, GLA_SLAB - GLA_COLS), w_in.dtype)], axis=1)
    r = lambda a: a.reshape(1, -1)
    return dict(
        norm_mix=r(P["norm_mix"][l]),
        w_gla=w_gla.astype(BF16),
        w_rwkv=w_in[:, GLA_COLS:].astype(BF16),
        gate_w2p=zpad(P["gla_gate_w2"][l], 0, LANES).astype(BF16),
        gate_b=r(P["gla_gate_b"][l]),
        gla_norm_w=r(P["gla_norm_w"][l]),
        mu=r(P["rwkv_mu"][l]),
        w0=r(P["rwkv_w0"][l]),
        w2p=zpad(P["rwkv_w2"][l], 0, LANES).astype(BF16),
        a0=r(P["rwkv_a0"][l]),
        a2p=zpad(P["rwkv_a2"][l], RWKV_W_LR, LANES).astype(BF16),
        g2=P["rwkv_g2"][l].astype(BF16),
        k_k=r(P["rwkv_k_k"][l]), k_a=r(P["rwkv_k_a"][l]), r_k=r(P["rwkv_r_k"][l]),
        ln_w=r(P["rwkv_ln_w"][l]), ln_b=r(P["rwkv_ln_b"][l]),
        w_out=P["w_out"][l].astype(BF16),
        norm_ffn=r(P["norm_ffn"][l]),
        w_up=P["ffn_w_up"][l].astype(BF16),
        conv_w=P["ffn_conv_w"][l], conv_b=r(P["ffn_conv_b"][l]),
        w_down=P["ffn_w_down"][l].astype(BF16),
        norm_final=r(P["norm_final"]),
    )


def _gla_state_in(s):
    st = jnp.swapaxes(s, -1, -2)
    z = jnp.zeros_like(st)
    par = (jnp.arange(GLA_HEADS) % 2).reshape(1, GLA_HEADS, 1, 1)
    return jnp.where(par == 0, jnp.concatenate([st, z], -1), jnp.concatenate([z, st], -1))


def _gla_state_out(st):
    lo, hi = st[..., :GLA_DK], st[..., GLA_DK:]
    par = (jnp.arange(GLA_HEADS) % 2).reshape(1, GLA_HEADS, 1, 1)
    return jnp.swapaxes(jnp.where(par == 0, lo, hi), -1, -2)


def _rwkv_state_in(s):
    b = s.shape[0]
    s = s.reshape(b, RWKV_PAIRS, 2, RWKV_N, RWKV_N)
    z = jnp.zeros_like(s[:, :, 0])
    top = jnp.concatenate([s[:, :, 0], z], -1)
    bot = jnp.concatenate([z, s[:, :, 1]], -1)
    return jnp.concatenate([top, bot], -2)


def _rwkv_state_out(sb):
    b = sb.shape[0]
    s0 = sb[:, :, :RWKV_N, :RWKV_N]
    s1 = sb[:, :, RWKV_N:, RWKV_N:]
    return jnp.stack([s0, s1], axis=2).reshape(b, RWKV_HEADS, RWKV_N, RWKV_N)


def _tiles(n_seqs, seq_len):
    tokens = n_seqs * seq_len
    tm = min(512, tokens)
    assert seq_len % tm == 0 or tm % seq_len == 0
    tm_in = min(2 * tm, tokens)
    assert tokens % tm_in == 0
    tl = min(256, seq_len)
    return tm_in, tm, tl


def _trunk(x, s_gla, s_rwkv, s_shift, c_conv, layers):
    b, l, _ = x.shape
    tm_in, tm, tl = _tiles(b, l)
    x2d = x.reshape(b * l, D_MODEL)
    new_gla, new_rwkv, new_shift, new_conv = [], [], [], []
    for li, prm in enumerate(layers):
        p_gla, p_rwkv = _inproj(x2d, prm["norm_mix"], prm["w_gla"], prm["w_rwkv"], tm_in)
        o_gla, st = _gla(p_gla.reshape(b, l, GLA_SLAB), prm, _gla_state_in(s_gla[li]), tl)
        y_rwkv, sb, sh = _rwkv(p_rwkv.reshape(b, l, RWKV_COLS), s_shift[li][:, None, :], prm,
                               _rwkv_state_in(s_rwkv[li]), tl)
        x2d, cc = _post(x2d, o_gla.reshape(b * l, GLA_V), y_rwkv.reshape(b * l, RWKV_C), prm,
                        c_conv[li], tm, l, li == len(layers) - 1)
        new_gla.append(_gla_state_out(st))
        new_rwkv.append(_rwkv_state_out(sb))
        new_shift.append(sh[:, 0, :])
        new_conv.append(cc)
    return (x2d.reshape(b, l, D_MODEL), jnp.stack(new_gla), jnp.stack(new_rwkv),
            jnp.stack(new_shift), jnp.stack(new_conv))


def kernel(x_prompt, x_sample, state_gla, state_rwkv, state_rwkv_shift, cache_ffn_conv, norm_mix, w_in, gla_gate_w2, gla_gate_b, gla_norm_w, rwkv_mu, rwkv_w0, rwkv_w2, rwkv_a0, rwkv_a2, rwkv_g2, rwkv_k_k, rwkv_k_a, rwkv_r_k, rwkv_ln_w, rwkv_ln_b, w_out, norm_ffn, ffn_w_up, ffn_conv_w, ffn_conv_b, ffn_w_down, norm_final):
    P = dict(norm_mix=norm_mix, w_in=w_in, gla_gate_w2=gla_gate_w2, gla_gate_b=gla_gate_b,
             gla_norm_w=gla_norm_w, rwkv_mu=rwkv_mu, rwkv_w0=rwkv_w0, rwkv_w2=rwkv_w2,
             rwkv_a0=rwkv_a0, rwkv_a2=rwkv_a2, rwkv_g2=rwkv_g2, rwkv_k_k=rwkv_k_k,
             rwkv_k_a=rwkv_k_a, rwkv_r_k=rwkv_r_k.reshape(DEPTH, RWKV_C), rwkv_ln_w=rwkv_ln_w,
             rwkv_ln_b=rwkv_ln_b, w_out=w_out, norm_ffn=norm_ffn, ffn_w_up=ffn_w_up,
             ffn_conv_w=ffn_conv_w, ffn_conv_b=ffn_conv_b, ffn_w_down=ffn_w_down,
             norm_final=norm_final)
    layers = [_layer_params(l, P) for l in range(DEPTH)]
    nb = x_prompt.shape[0]
    g0 = jnp.zeros((DEPTH, nb) + state_gla.shape[2:], state_gla.dtype)
    r0 = jnp.zeros((DEPTH, nb) + state_rwkv.shape[2:], state_rwkv.dtype)
    sh0 = jnp.zeros((DEPTH, nb) + state_rwkv_shift.shape[2:], state_rwkv_shift.dtype)
    c0 = jnp.zeros((DEPTH, nb) + cache_ffn_conv.shape[2:], cache_ffn_conv.dtype)
    y_p, gla_p, rwkv_p, shift_p, conv_p = _trunk(x_prompt, g0, r0, sh0, c0, layers)
    y_s, gla_s, rwkv_s, shift_s, conv_s = _trunk(
        x_sample, state_gla, state_rwkv, state_rwkv_shift, cache_ffn_conv, layers)
    return (y_p, y_s, gla_p, rwkv_p, shift_p, conv_p, gla_s, rwkv_s, shift_s, conv_s)
```

```python
import functools
import math

import jax
import jax.numpy as jnp
from jax import lax
from jax.experimental import pallas as pl
from jax.experimental.pallas import tpu as pltpu

F32 = jnp.float32
BF16 = jnp.bfloat16

D_MODEL = 1024
DEPTH = 2
CHUNK = 64
NORM_EPS = 1e-6
GLA_HEADS = 4
GLA_DV = 128
GLA_DK = 64
GLA_K = GLA_HEADS * GLA_DK
GLA_V = GLA_HEADS * GLA_DV
GLA_LR = 16
GLA_GATE_NORMALIZER = 16.0
RWKV_N = 64
RWKV_C = 512
RWKV_HEADS = 8
RWKV_PAIRS = RWKV_HEADS // 2
RWKV_W_LR = 64
RWKV_A_LR = 64
RWKV_G_LR = 128
RWKV_GN_EPS = 64e-5
GLA_COLS = 2 * GLA_K + 2 * GLA_V + GLA_LR
RWKV_COLS = 3 * RWKV_C + RWKV_W_LR + RWKV_A_LR + RWKV_G_LR
D_FF = 2816
CONV_W = 3

LANES = 128
MXU_TILE = 256
GLA_SLAB = 2 * GLA_K + 2 * GLA_V + LANES
VMEM_LIMIT_BYTES = 56 * 1024 * 1024
MIXER_SEQS_PER_STEP = 4
RWKV_STREAM_LAG = 8
GLA_STREAM_LAG = 3
POST_SUBTILES = 2
POST_STREAM_LAG = 2

_GQ, _GK, _GV, _GG, _GLR = 0, GLA_K, 2 * GLA_K, 2 * GLA_K + GLA_V, 2 * GLA_K + 2 * GLA_V
_RR, _RK, _RV, _RWA, _RG = 0, RWKV_C, 2 * RWKV_C, 3 * RWKV_C, 3 * RWKV_C + RWKV_W_LR + RWKV_A_LR


def _bdot(a, b):
    return jnp.dot(a.astype(BF16), b.astype(BF16), preferred_element_type=F32)


def _bdot_nt(a, b):
    return lax.dot_general(a.astype(BF16), b.astype(BF16), (((1,), (1,)), ((), ())),
                           preferred_element_type=F32)


def _bdot_tn(a, b):
    return lax.dot_general(a.astype(BF16), b.astype(BF16), (((0,), (0,)), ((), ())),
                           preferred_element_type=F32)


def _split3(x):
    hi = x.astype(BF16)
    r1 = x - hi.astype(F32)
    mid = r1.astype(BF16)
    lo = (r1 - mid.astype(F32)).astype(BF16)
    return hi, mid, lo


def _exact_dot(m_bf16, x, pieces=3):
    d = lambda p: jnp.dot(m_bf16, p, preferred_element_type=F32)
    if pieces == 2:
        hi = x.astype(BF16)
        return d(hi) + d((x - hi.astype(F32)).astype(BF16))
    hi, mid, lo = _split3(x)
    return d(hi) + d(mid) + d(lo)


def _head_sum(x):
    r = lax.broadcasted_iota(jnp.int32, (MXU_TILE, MXU_TILE), 0)
    c = lax.broadcasted_iota(jnp.int32, (MXU_TILE, MXU_TILE), 1)
    ones_blk = ((r // RWKV_N) == (c // RWKV_N)).astype(BF16)
    xb = x.astype(BF16)
    rows, groups = x.shape[0], x.shape[1] // MXU_TILE
    stacked = jnp.concatenate([xb[:, g * MXU_TILE:(g + 1) * MXU_TILE] for g in range(groups)], axis=0)
    out = jnp.dot(stacked, ones_blk, preferred_element_type=F32)
    return jnp.concatenate([out[g * rows:(g + 1) * rows] for g in range(groups)], axis=1)


def _tril_incl(n):
    r = lax.broadcasted_iota(jnp.int32, (n, n), 0)
    c = lax.broadcasted_iota(jnp.int32, (n, n), 1)
    return (c <= r)


def _rmsnorm(x, g):
    return x * lax.rsqrt(jnp.mean(x * x, axis=-1, keepdims=True) + NORM_EPS) * g


def _interleave(streams, lag):
    active, waiting, rnd = [], list(streams), 0
    while active or waiting:
        if waiting and rnd % lag == 0:
            active.append(waiting.pop(0))
        for g in list(active):
            if next(g, _DONE) is _DONE:
                active.remove(g)
        rnd += 1


_DONE = object()


def _shift_rows(x, k, fill_rows):
    axis = x.ndim - 2
    y = pltpu.roll(x, k, axis)
    row = lax.broadcasted_iota(jnp.int32, x.shape, axis)
    for i, f in enumerate(fill_rows):
        y = jnp.where(row == i, f, y)
    return y


def _inproj_kernel(x_ref, g_ref, wg_ref, wr_ref, og_ref, or_ref):
    h = _rmsnorm(x_ref[...], g_ref[...]).astype(BF16)
    og_ref[...] = jnp.dot(h, wg_ref[...], preferred_element_type=F32)
    or_ref[...] = jnp.dot(h, wr_ref[...], preferred_element_type=F32)


def _inproj(x2d, g, w_gla, w_rwkv, tm):
    t = x2d.shape[0]
    const = lambda m, n: pl.BlockSpec((m, n), lambda i: (0, 0), pipeline_mode=pl.Buffered(1))
    return pl.pallas_call(
        _inproj_kernel,
        grid=(t // tm,),
        in_specs=[
            pl.BlockSpec((tm, D_MODEL), lambda i: (i, 0)),
            const(1, D_MODEL),
            const(D_MODEL, GLA_SLAB),
            const(D_MODEL, RWKV_COLS),
        ],
        out_specs=[
            pl.BlockSpec((tm, GLA_SLAB), lambda i: (i, 0)),
            pl.BlockSpec((tm, RWKV_COLS), lambda i: (i, 0)),
        ],
        out_shape=[
            jax.ShapeDtypeStruct((t, GLA_SLAB), F32),
            jax.ShapeDtypeStruct((t, RWKV_COLS), F32),
        ],
        compiler_params=pltpu.CompilerParams(
            dimension_semantics=("arbitrary",), vmem_limit_bytes=VMEM_LIMIT_BYTES),
        name="inproj",
    )(x2d, g, w_gla, w_rwkv)


def _gla_streams(p_ref, w2_ref, b_ref, nw_ref, s0_ref, o_ref, sout_ref, st_ref, *, n_chunks, nb):
    ti = pl.program_id(1)

    @pl.when(ti == 0)
    def _():
        st_ref[...] = s0_ref[...]

    tl = n_chunks * CHUNK
    ri = lax.broadcasted_iota(jnp.int32, (tl, tl), 0)
    ci = lax.broadcasted_iota(jnp.int32, (tl, tl), 1)
    causal = ((ri // CHUNK) == (ci // CHUNK)) & (ci <= ri)
    causal_bf = causal.astype(BF16)
    lane = lax.broadcasted_iota(jnp.int32, (tl, LANES), 1)
    half = [lane < GLA_DK, lane >= GLA_DK]
    nw = nw_ref[...]
    heads = range(GLA_HEADS)
    chunks = range(n_chunks)
    crow = lambda x, c: x[c * CHUNK:(c + 1) * CHUNK]
    pair = lambda x, h: x[:, (h // 2) * LANES:(h // 2 + 1) * LANES]

    def stream(bi):
        logit = _bdot(p_ref[bi, :, _GLR:_GLR + LANES], w2_ref[...]) + b_ref[...]
        la = jax.nn.log_sigmoid(logit) / GLA_GATE_NORMALIZER
        yield
        cum = _exact_dot(causal_bf, la, pieces=2)
        tot = jnp.concatenate(
            [jnp.broadcast_to(cum[(c + 1) * CHUNK - 1:(c + 1) * CHUNK], (CHUNK, GLA_K))
             for c in chunks], axis=0)
        q = p_ref[bi, :, _GQ:_GQ + GLA_K] * (GLA_DK ** -0.5)
        k = p_ref[bi, :, _GK:_GK + GLA_K]
        yield
        q_e = q * jnp.exp(cum)
        k_e = k * jnp.exp(-cum)
        k_hat = k * jnp.exp(tot - cum)
        dec = jnp.exp(tot)
        v = [p_ref[bi, :, _GV + h * GLA_DV:_GV + (h + 1) * GLA_DV] for h in heads]
        yield

        sc_pair = [_bdot_nt(jnp.concatenate([jnp.where(half[0], pair(q_e, 2 * p), 0.0),
                                             jnp.where(half[1], pair(q_e, 2 * p), 0.0)], axis=0),
                            pair(k_e, 2 * p)) for p in range(GLA_HEADS // 2)]
        sc = [sc_pair[h // 2][(h % 2) * tl:(h % 2 + 1) * tl] for h in heads]
        yield
        o_intra = [_bdot(jnp.where(causal, sc[h], 0.0), v[h]) for h in heads]
        yield

        kh_m = [jnp.where(half[h % 2], pair(k_hat, h), 0.0) for h in heads]
        upd = [[_bdot_tn(crow(v[h], c), crow(kh_m[h], c)) for c in chunks] for h in heads]
        yield
        st_in = []
        for h in heads:
            st = st_ref[bi, h]
            per_chunk = []
            for c in chunks:
                per_chunk.append(st)
                st = st * pair(dec, h)[c * CHUNK:c * CHUNK + 1] + upd[h][c]
            st_ref[bi, h] = st
            st_in.append(per_chunk)
        o_inter = [jnp.concatenate([_bdot_nt(crow(pair(q_e, h), c), st_in[h][c]) for c in chunks],
                                   axis=0) for h in heads]
        yield

        for h in heads:
            o = o_intra[h] + o_inter[h]
            gg = p_ref[bi, :, _GG + h * GLA_DV:_GG + (h + 1) * GLA_DV]
            o = o * lax.rsqrt(jnp.mean(o * o, axis=-1, keepdims=True) + NORM_EPS) * nw
            o = o * (gg * jax.nn.sigmoid(gg))
            o_ref[bi, :, h * GLA_DV:(h + 1) * GLA_DV] = o.astype(o_ref.dtype)
            if h % 2 == 1:
                yield
        sout_ref[bi] = st_ref[bi]

    return [stream(bi) for bi in range(nb)]


def _rwkv_streams(p_ref, sh_ref, mu_ref, w0_ref, w2_ref, a0_ref, a2_ref, g2_ref, kk_ref, ka_ref,
                  rk_ref, lnw_ref, lnb_ref, s0_ref,
                  y_ref, sout_ref, shout_ref,
                  s_ref, prev_ref, *, n_chunks, tl, nb):
    ti = pl.program_id(1)

    @pl.when(ti == 0)
    def _():
        s_ref[...] = s0_ref[...]
        prev_ref[...] = sh_ref[...]

    seg_sum = _head_sum
    tr = lax.broadcasted_iota(jnp.int32, (tl, tl), 0)
    tc = lax.broadcasted_iota(jnp.int32, (tl, tl), 1)
    chunk_tril = (((tr // CHUNK) == (tc // CHUNK)) & (tc <= tr)).astype(BF16)

    trow = lax.broadcasted_iota(jnp.int32, (CHUNK, LANES), 0)
    tcol = lax.broadcasted_iota(jnp.int32, (CHUNK, LANES), 1) & (RWKV_N - 1)
    strict = tcol < trow
    incl = tcol <= trow
    eye = (tcol == trow).astype(F32)
    ri = lax.broadcasted_iota(jnp.int32, (LANES, LANES), 0)
    ci = lax.broadcasted_iota(jnp.int32, (LANES, LANES), 1)
    same_head = (ri >> 6) == (ci >> 6)
    zeros_bd = jnp.zeros((LANES, LANES), BF16)

    def blockdiag(x):
        xb = x.astype(BF16)
        return jnp.where(same_head, jnp.concatenate([xb, xb], axis=0), zeros_bd)

    def stream(bi):
        p = p_ref[bi]
        prev = _shift_rows(p, 1, [prev_ref[bi]])
        prev_ref[bi] = p[tl - 1:tl, :]
        shout_ref[bi] = p[tl - 1:tl, :]
        xs = p + mu_ref[...] * (prev - p)
        xr = xs[:, _RR:_RR + RWKV_C]
        xk = xs[:, _RK:_RK + RWKV_C]
        xv = xs[:, _RV:_RV + RWKV_C]
        xwa = xs[:, _RWA:_RWA + LANES]
        xg = xs[:, _RG:_RG + RWKV_G_LR]
        yield
        w_raw = w0_ref[...] + _bdot(jnp.tanh(xwa), w2_ref[...])
        w_log = -(math.exp(-0.5)) * jax.nn.sigmoid(w_raw)
        a = jax.nn.sigmoid(a0_ref[...] + _bdot(xwa, a2_ref[...]))
        g = _bdot(jax.nn.sigmoid(xg), g2_ref[...])
        yield
        cl_tile = _exact_dot(chunk_tril, w_log, pieces=2)
        kk = xk * kk_ref[...]
        kr = xk * (1.0 + (a - 1.0) * ka_ref[...])
        sums = seg_sum(jnp.concatenate([kk * kk, xr * kr * rk_ref[...]], axis=0))
        yield
        kk = kk * lax.rsqrt(jnp.maximum(sums[:tl], 1e-24))
        bonus = sums[tl:] * xv
        a_vec = -kk
        b_vec = kk * a
        yield

        units = [(slice(c * CHUNK, (c + 1) * CHUNK), slice(pr * LANES, (pr + 1) * LANES))
                 for c in range(n_chunks) for pr in range(RWKV_PAIRS)]
        n = range(len(units))
        a_t, r_t, b_t, k_t, b_h, k_h, v, dec = [], [], [], [], [], [], [], []
        for ui, (rows, cols) in enumerate(units):
            lw, cl = w_log[rows, cols], cl_tile[rows, cols]
            dec_row = jnp.exp(cl[CHUNK - 1:CHUNK, :])
            e_neg = jnp.exp(-cl)
            e_end = dec_row * e_neg
            av, bv, kv = a_vec[rows, cols], b_vec[rows, cols], kr[rows, cols]
            a_t.append(av * jnp.exp(cl - lw))
            r_t.append(xr[rows, cols] * jnp.exp(cl))
            b_t.append(bv * e_neg)
            k_t.append(kv * e_neg)
            b_h.append(bv * e_end)
            k_h.append(kv * e_end)
            v.append(xv[rows, cols])
            dec.append(dec_row)
            if ui % RWKV_PAIRS == RWKV_PAIRS - 1:
                yield
        gm = [_bdot_nt(jnp.concatenate([a_t[i], r_t[i]], axis=0),
                       jnp.concatenate([blockdiag(b_t[i]), blockdiag(k_t[i])], axis=0)) for i in n]
        yield
        a_ab = [jnp.where(strict, gm[i][:CHUNK, :LANES], 0.0) for i in n]
        a_ak = [jnp.where(strict, gm[i][:CHUNK, LANES:], 0.0) for i in n]
        r_b = [jnp.where(incl, gm[i][CHUNK:, :LANES], 0.0) for i in n]
        r_k = [jnp.where(incl, gm[i][CHUNK:, LANES:], 0.0) for i in n]
        inv = [eye + a_ab[i] for i in n]
        pw = [_bdot(a_ab[i], blockdiag(a_ab[i])) for i in n]
        yield
        for _ in range(4):
            both = [_bdot(jnp.concatenate([inv[i], pw[i]], axis=0), blockdiag(pw[i])) for i in n]
            inv = [inv[i] + both[i][:CHUNK] for i in n]
            pw = [both[i][CHUNK:] for i in n]
            yield
        inv = [inv[i] + _bdot(inv[i], blockdiag(pw[i])) for i in n]
        yield
        v_bd = [blockdiag(v[i]) for i in n]
        ark = [_bdot(jnp.concatenate([a_ak[i], r_k[i]], axis=0), v_bd[i]) for i in n]
        akv = [ark[i][:CHUNK] for i in n]
        rkv = [ark[i][CHUNK:] for i in n]
        yield
        w = [_bdot(inv[i], jnp.concatenate([blockdiag(a_t[i]), blockdiag(akv[i])], axis=1))
             for i in n]
        a_bar = [w[i][:, :LANES] for i in n]
        u0 = [w[i][:, LANES:] for i in n]
        yield
        ar = [jnp.concatenate([a_bar[i], r_t[i]], axis=0) for i in n]
        bk_h = [jnp.concatenate([b_h[i], k_h[i]], axis=0) for i in n]
        pairs = range(RWKV_PAIRS)
        y_chunks = []
        for c in range(n_chunks):
            ids = [c * RWKV_PAIRS + pr for pr in pairs]
            s = [s_ref[bi, pr] for pr in pairs]
            ars = [_bdot_nt(ar[i], s[pr]) for pr, i in zip(pairs, ids)]
            u = [ars[pr][:CHUNK] + u0[i] for pr, i in zip(pairs, ids)]
            yield
            upd = [_bdot_tn(jnp.concatenate([u[pr], v[i]], axis=0), bk_h[i])
                   for pr, i in zip(pairs, ids)]
            y_pairs = [ars[pr][CHUNK:] + rkv[i] + _bdot(r_b[i], blockdiag(u[pr]))
                       for pr, i in zip(pairs, ids)]
            for pr, i in zip(pairs, ids):
                s_ref[bi, pr] = s[pr] * dec[i] + jnp.where(same_head, upd[pr], 0.0)
            y_chunks.append(jnp.concatenate(y_pairs, axis=1))
            yield
        sout_ref[bi] = s_ref[bi]

        y = jnp.concatenate(y_chunks, axis=0)
        mean = seg_sum(y) * (1.0 / RWKV_N)
        yc = y - mean
        yield
        var = seg_sum(yc * yc) * (1.0 / RWKV_N)
        y = yc * lax.rsqrt(var + RWKV_GN_EPS) * lnw_ref[...] + lnb_ref[...]
        y_ref[bi] = ((y + bonus) * g).astype(y_ref.dtype)

    return [stream(bi) for bi in range(nb)]


def _gla_kernel(*refs, n_chunks, nb, lag):
    _interleave(_gla_streams(*refs, n_chunks=n_chunks, nb=nb), lag)


def _rwkv_kernel(*refs, n_chunks, tl, nb, lag):
    _interleave(_rwkv_streams(*refs, n_chunks=n_chunks, tl=tl, nb=nb), lag)


def _gla(p_gla, prm, st0, tl):
    b, l, _ = p_gla.shape
    nb = MIXER_SEQS_PER_STEP
    assert b % nb == 0 and l % tl == 0
    row = lambda n: pl.BlockSpec((1, n), lambda i, j: (0, 0))
    seq = lambda *dims: pl.BlockSpec((nb,) + dims, lambda i, j: (i,) + (0,) * len(dims))
    tile = lambda n: pl.BlockSpec((nb, tl, n), lambda i, j: (i, j, 0))
    return pl.pallas_call(
        functools.partial(_gla_kernel, n_chunks=tl // CHUNK, nb=nb, lag=GLA_STREAM_LAG),
        grid=(b // nb, l // tl),
        in_specs=[
            tile(GLA_SLAB), pl.BlockSpec((LANES, GLA_K), lambda i, j: (0, 0)), row(GLA_K),
            row(GLA_DV), seq(GLA_HEADS, GLA_DV, LANES),
        ],
        out_specs=[tile(GLA_V), seq(GLA_HEADS, GLA_DV, LANES)],
        out_shape=[
            jax.ShapeDtypeStruct((b, l, GLA_V), BF16),
            jax.ShapeDtypeStruct((b, GLA_HEADS, GLA_DV, LANES), F32),
        ],
        scratch_shapes=[pltpu.VMEM((nb, GLA_HEADS, GLA_DV, LANES), F32)],
        compiler_params=pltpu.CompilerParams(
            dimension_semantics=("arbitrary", "arbitrary"), vmem_limit_bytes=VMEM_LIMIT_BYTES),
        name="gla",
    )(p_gla, prm["gate_w2p"], prm["gate_b"], prm["gla_norm_w"], st0)


def _rwkv(p_rwkv, shift_prev, prm, s0, tl):
    b, l, _ = p_rwkv.shape
    nb = MIXER_SEQS_PER_STEP
    assert b % nb == 0 and l % tl == 0
    row = lambda n: pl.BlockSpec((1, n), lambda i, j: (0, 0))
    mat = lambda m, n: pl.BlockSpec((m, n), lambda i, j: (0, 0))
    seq = lambda *dims: pl.BlockSpec((nb,) + dims, lambda i, j: (i,) + (0,) * len(dims))
    tile = lambda n: pl.BlockSpec((nb, tl, n), lambda i, j: (i, j, 0))
    return pl.pallas_call(
        functools.partial(_rwkv_kernel, n_chunks=tl // CHUNK, tl=tl, nb=nb, lag=RWKV_STREAM_LAG),
        grid=(b // nb, l // tl),
        in_specs=[
            tile(RWKV_COLS), seq(1, RWKV_COLS), row(RWKV_COLS),
            row(RWKV_C), mat(LANES, RWKV_C), row(RWKV_C), mat(LANES, RWKV_C), mat(RWKV_G_LR, RWKV_C),
            row(RWKV_C), row(RWKV_C), row(RWKV_C), row(RWKV_C), row(RWKV_C),
            seq(RWKV_PAIRS, LANES, LANES),
        ],
        out_specs=[tile(RWKV_C), seq(RWKV_PAIRS, LANES, LANES), seq(1, RWKV_COLS)],
        out_shape=[
            jax.ShapeDtypeStruct((b, l, RWKV_C), BF16),
            jax.ShapeDtypeStruct((b, RWKV_PAIRS, LANES, LANES), F32),
            jax.ShapeDtypeStruct((b, 1, RWKV_COLS), F32),
        ],
        scratch_shapes=[
            pltpu.VMEM((nb, RWKV_PAIRS, LANES, LANES), F32),
            pltpu.VMEM((nb, 1, RWKV_COLS), F32),
        ],
        compiler_params=pltpu.CompilerParams(
            dimension_semantics=("arbitrary", "arbitrary"), vmem_limit_bytes=VMEM_LIMIT_BYTES),
        name="rwkv",
    )(p_rwkv, shift_prev, prm["mu"], prm["w0"], prm["w2p"], prm["a0"], prm["a2p"], prm["g2"],
      prm["k_k"], prm["k_a"], prm["r_k"], prm["ln_w"], prm["ln_b"], s0)


def _post_kernel(x_ref, og_ref, yr_ref, wo_ref, nf_ref, wup_ref, cw_ref, cb_ref, wd_ref,
                 cprev_ref, nfin_ref, out_ref, cnew_ref, act_ref, carry_ref,
                 *, tiles_per_seq, seqs_per_tile, tm, final_norm):
    i = pl.program_id(0)

    @pl.when(i % tiles_per_seq == 0)
    def _():
        carry_ref[...] = cprev_ref[...]

    ts = tm // POST_SUBTILES
    seqs_sub = max(seqs_per_tile // POST_SUBTILES, 1)
    rows = ts // seqs_sub

    def stream(sub):
        r = slice(sub * ts, (sub + 1) * ts)
        sq = slice(sub * seqs_sub, (sub + 1) * seqs_sub) if seqs_per_tile > 1 else slice(0, 1)
        wo = wo_ref[...]
        mix = (jnp.dot(og_ref[r, :], wo[:GLA_V], preferred_element_type=F32)
               + jnp.dot(yr_ref[r, :], wo[GLA_V:], preferred_element_type=F32))
        x1 = x_ref[r, :] + mix
        out_ref[r, :] = x1
        h = _rmsnorm(x1, nf_ref[...]).astype(BF16)
        yield
        for j in range(D_FF // MXU_TILE):
            cols = slice(j * MXU_TILE, (j + 1) * MXU_TILE)
            gcols = slice(D_FF + j * MXU_TILE, D_FF + (j + 1) * MXU_TILE)
            u = jnp.dot(h, wup_ref[:, cols], preferred_element_type=F32)
            gate = jnp.dot(h, wup_ref[:, gcols], preferred_element_type=F32)
            u3 = u.reshape(seqs_sub, rows, MXU_TILE)
            p0, p1 = carry_ref[sq, 0:1, cols], carry_ref[sq, 1:2, cols]
            uc = (cb_ref[:, cols] + cw_ref[0:1, cols] * _shift_rows(u3, 2, [p0, p1])
                  + cw_ref[1:2, cols] * _shift_rows(u3, 1, [p1]) + cw_ref[2:3, cols] * u3)
            uc = uc.reshape(ts, MXU_TILE)
            new_prev = u3[:, rows - 2:rows, :]
            carry_ref[sq, :, cols] = new_prev
            cnew_ref[sq, :, cols] = new_prev
            act = 0.5 * uc * (1.0 + lax.erf(uc * (2.0 ** -0.5))) * gate
            act_ref[r, cols] = act.astype(BF16)
            yield
        x2 = out_ref[r, :] + jnp.dot(act_ref[r, :], wd_ref[...], preferred_element_type=F32)
        if final_norm:
            x2 = _rmsnorm(x2, nfin_ref[...])
        out_ref[r, :] = x2

    _interleave([stream(sub) for sub in range(POST_SUBTILES)], POST_STREAM_LAG)


def _post(x2d, o_gla, y_rwkv, prm, conv_prev, tm, seq_len, final_norm):
    t = x2d.shape[0]
    b = conv_prev.shape[0]
    tiles_per_seq = max(seq_len // tm, 1)
    spt = max(tm // seq_len, 1)
    const = lambda m, n: pl.BlockSpec((m, n), lambda i: (0, 0), pipeline_mode=pl.Buffered(1))
    return pl.pallas_call(
        functools.partial(_post_kernel, tiles_per_seq=tiles_per_seq, seqs_per_tile=spt, tm=tm,
                          final_norm=final_norm),
        grid=(t // tm,),
        in_specs=[
            pl.BlockSpec((tm, D_MODEL), lambda i: (i, 0)),
            pl.BlockSpec((tm, GLA_V), lambda i: (i, 0)),
            pl.BlockSpec((tm, RWKV_C), lambda i: (i, 0)),
            const(D_MODEL, D_MODEL),
            const(1, D_MODEL),
            const(D_MODEL, 2 * D_FF),
            const(CONV_W, D_FF),
            const(1, D_FF),
            const(D_FF, D_MODEL),
            pl.BlockSpec((spt, CONV_W - 1, D_FF), lambda i: (i // tiles_per_seq, 0, 0)),
            const(1, D_MODEL),
        ],
        out_specs=[
            pl.BlockSpec((tm, D_MODEL), lambda i: (i, 0)),
            pl.BlockSpec((spt, CONV_W - 1, D_FF), lambda i: (i // tiles_per_seq, 0, 0)),
        ],
        out_shape=[
            jax.ShapeDtypeStruct((t, D_MODEL), F32),
            jax.ShapeDtypeStruct((b, CONV_W - 1, D_FF), F32),
        ],
        scratch_shapes=[
            pltpu.VMEM((tm, D_FF), BF16),
            pltpu.VMEM((spt, CONV_W - 1, D_FF), F32),
        ],
        compiler_params=pltpu.CompilerParams(
            dimension_semantics=("arbitrary",), vmem_limit_bytes=VMEM_LIMIT_BYTES),
        name="post",
    )(x2d, o_gla, y_rwkv, prm["w_out"], prm["norm_ffn"], prm["w_up"],
      prm["conv_w"], prm["conv_b"], prm["w_down"], conv_prev, prm["norm_final"])


def _layer_params(l, P):
    w_in = P["w_in"][l]
    zpad = lambda a, rows_before, rows_total: jnp.zeros((rows_total, a.shape[1]), a.dtype).at[
        rows_before:rows_before + a.shape[0]].set(a)
    w_gla = jnp.concatenate(
        [w_in[:, :GLA_COLS], jnp.zeros((D_MODEL, GLA_SLAB - GLA_COLS), w_in.dtype)], axis=1)
    r = lambda a: a.reshape(1, -1)
    return dict(
        norm_mix=r(P["norm_mix"][l]),
        w_gla=w_gla.astype(BF16),
        w_rwkv=w_in[:, GLA_COLS:].astype(BF16),
        gate_w2p=zpad(P["gla_gate_w2"][l], 0, LANES).astype(BF16),
        gate_b=r(P["gla_gate_b"][l]),
        gla_norm_w=r(P["gla_norm_w"][l]),
        mu=r(P["rwkv_mu"][l]),
        w0=r(P["rwkv_w0"][l]),
        w2p=zpad(P["rwkv_w2"][l], 0, LANES).astype(BF16),
        a0=r(P["rwkv_a0"][l]),
        a2p=zpad(P["rwkv_a2"][l], RWKV_W_LR, LANES).astype(BF16),
        g2=P["rwkv_g2"][l].astype(BF16),
        k_k=r(P["rwkv_k_k"][l]), k_a=r(P["rwkv_k_a"][l]), r_k=r(P["rwkv_r_k"][l]),
        ln_w=r(P["rwkv_ln_w"][l]), ln_b=r(P["rwkv_ln_b"][l]),
        w_out=P["w_out"][l].astype(BF16),
        norm_ffn=r(P["norm_ffn"][l]),
        w_up=P["ffn_w_up"][l].astype(BF16),
        conv_w=P["ffn_conv_w"][l], conv_b=r(P["ffn_conv_b"][l]),
        w_down=P["ffn_w_down"][l].astype(BF16),
        norm_final=r(P["norm_final"]),
    )


def _gla_state_in(s):
    st = jnp.swapaxes(s, -1, -2)
    z = jnp.zeros_like(st)
    par = (jnp.arange(GLA_HEADS) % 2).reshape(1, GLA_HEADS, 1, 1)
    return jnp.where(par == 0, jnp.concatenate([st, z], -1), jnp.concatenate([z, st], -1))


def _gla_state_out(st):
    lo, hi = st[..., :GLA_DK], st[..., GLA_DK:]
    par = (jnp.arange(GLA_HEADS) % 2).reshape(1, GLA_HEADS, 1, 1)
    return jnp.swapaxes(jnp.where(par == 0, lo, hi), -1, -2)


def _rwkv_state_in(s):
    b = s.shape[0]
    s = s.reshape(b, RWKV_PAIRS, 2, RWKV_N, RWKV_N)
    z = jnp.zeros_like(s[:, :, 0])
    top = jnp.concatenate([s[:, :, 0], z], -1)
    bot = jnp.concatenate([z, s[:, :, 1]], -1)
    return jnp.concatenate([top, bot], -2)


def _rwkv_state_out(sb):
    b = sb.shape[0]
    s0 = sb[:, :, :RWKV_N, :RWKV_N]
    s1 = sb[:, :, RWKV_N:, RWKV_N:]
    return jnp.stack([s0, s1], axis=2).reshape(b, RWKV_HEADS, RWKV_N, RWKV_N)


def _tiles(n_seqs, seq_len):
    tokens = n_seqs * seq_len
    tm = min(512, tokens)
    assert seq_len % tm == 0 or tm % seq_len == 0
    tm_in = min(2 * tm, tokens)
    assert tokens % tm_in == 0
    tl = min(256, seq_len)
    return tm_in, tm, tl


def _trunk(x, s_gla, s_rwkv, s_shift, c_conv, layers):
    b, l, _ = x.shape
    tm_in, tm, tl = _tiles(b, l)
    x2d = x.reshape(b * l, D_MODEL)
    new_gla, new_rwkv, new_shift, new_conv = [], [], [], []
    for li, prm in enumerate(layers):
        p_gla, p_rwkv = _inproj(x2d, prm["norm_mix"], prm["w_gla"], prm["w_rwkv"], tm_in)
        o_gla, st = _gla(p_gla.reshape(b, l, GLA_SLAB), prm, _gla_state_in(s_gla[li]), tl)
        y_rwkv, sb, sh = _rwkv(p_rwkv.reshape(b, l, RWKV_COLS), s_shift[li][:, None, :], prm,
                               _rwkv_state_in(s_rwkv[li]), tl)
        x2d, cc = _post(x2d, o_gla.reshape(b * l, GLA_V), y_rwkv.reshape(b * l, RWKV_C), prm,
                        c_conv[li], tm, l, li == len(layers) - 1)
        new_gla.append(_gla_state_out(st))
        new_rwkv.append(_rwkv_state_out(sb))
        new_shift.append(sh[:, 0, :])
        new_conv.append(cc)
    return (x2d.reshape(b, l, D_MODEL), jnp.stack(new_gla), jnp.stack(new_rwkv),
            jnp.stack(new_shift), jnp.stack(new_conv))


def kernel(x_prompt, x_sample, state_gla, state_rwkv, state_rwkv_shift, cache_ffn_conv, norm_mix, w_in, gla_gate_w2, gla_gate_b, gla_norm_w, rwkv_mu, rwkv_w0, rwkv_w2, rwkv_a0, rwkv_a2, rwkv_g2, rwkv_k_k, rwkv_k_a, rwkv_r_k, rwkv_ln_w, rwkv_ln_b, w_out, norm_ffn, ffn_w_up, ffn_conv_w, ffn_conv_b, ffn_w_down, norm_final):
    P = dict(norm_mix=norm_mix, w_in=w_in, gla_gate_w2=gla_gate_w2, gla_gate_b=gla_gate_b,
             gla_norm_w=gla_norm_w, rwkv_mu=rwkv_mu, rwkv_w0=rwkv_w0, rwkv_w2=rwkv_w2,
             rwkv_a0=rwkv_a0, rwkv_a2=rwkv_a2, rwkv_g2=rwkv_g2, rwkv_k_k=rwkv_k_k,
             rwkv_k_a=rwkv_k_a, rwkv_r_k=rwkv_r_k.reshape(DEPTH, RWKV_C), rwkv_ln_w=rwkv_ln_w,
             rwkv_ln_b=rwkv_ln_b, w_out=w_out, norm_ffn=norm_ffn, ffn_w_up=ffn_w_up,
             ffn_conv_w=ffn_conv_w, ffn_conv_b=ffn_conv_b, ffn_w_down=ffn_w_down,
             norm_final=norm_final)
    layers = [_layer_params(l, P) for l in range(DEPTH)]
    nb = x_prompt.shape[0]
    g0 = jnp.zeros((DEPTH, nb) + state_gla.shape[2:], state_gla.dtype)
    r0 = jnp.zeros((DEPTH, nb) + state_rwkv.shape[2:], state_rwkv.dtype)
    sh0 = jnp.zeros((DEPTH, nb) + state_rwkv_shift.shape[2:], state_rwkv_shift.dtype)
    c0 = jnp.zeros((DEPTH, nb) + cache_ffn_conv.shape[2:], cache_ffn_conv.dtype)
    y_p, gla_p, rwkv_p, shift_p, conv_p = _trunk(x_prompt, g0, r0, sh0, c0, layers)
    y_s, gla_s, rwkv_s, shift_s, conv_s = _trunk(
        x_sample, state_gla, state_rwkv, state_rwkv_shift, cache_ffn_conv, layers)
    return (y_p, y_s, gla_p, rwkv_p, shift_p, conv_p, gla_s, rwkv_s, shift_s, conv_s)
```
